```python
import jax, jax.numpy as jnp
from jax import lax
import numpy as np

D_MODEL = 1024
BATCH = 4
SEQ = 4096
DEPTH = 2

GRID_W = 64
CTX_LEN = 256
EPS = 1e-6
NEG_INF = -1e30
N_MOD = 6
N_EVEN = (DEPTH + 1) // 2
N_ODD = DEPTH // 2
POOL_WINDOWS = (2, 4, 8, 16)
N_POOL_GROUPS = len(POOL_WINDOWS)
POOL_GROUP_DIM = D_MODEL // 8
POOL_WIDTH = N_POOL_GROUPS * POOL_GROUP_DIM
HEAD_DIM = 64
N_Q_HEADS = D_MODEL // 128
N_KV_HEADS = N_Q_HEADS // 4
GQA_GROUP = N_Q_HEADS // N_KV_HEADS
ATTN_WIDTH = N_Q_HEADS * HEAD_DIM
KV_WIDTH = N_KV_HEADS * HEAD_DIM
WINDOW = 128
ATTN_BLOCK = 128
ROPE_BASE = 10000.0
ROPE_FREQS = HEAD_DIM // 4
Q_END = POOL_WIDTH + ATTN_WIDTH
IN_EVEN = Q_END + 2 * KV_WIDTH
MIX_EVEN = POOL_WIDTH + ATTN_WIDTH
CHUNK = 128
N_SGU_GROUPS = 8
SGU_WIDTH = D_MODEL
SGU_GROUP_DIM = SGU_WIDTH // N_SGU_GROUPS
D_FF = 128 * ((8 * D_MODEL // 3 + 127) // 128)
CONV_W = 3

kernel_name = "hybrid_pool_swa_sgu_convffn_ctxprefix"

f32 = jnp.float32


def rmsnorm(x, g):
    xf = x.astype(f32)
    y = xf * lax.rsqrt(jnp.mean(xf * xf, axis=-1, keepdims=True) + EPS)
    return (y * g.astype(f32)).astype(x.dtype)


def layernorm(x, g, b):
    xf = x.astype(f32)
    mu = jnp.mean(xf, axis=-1, keepdims=True)
    var = jnp.mean(jnp.square(xf - mu), axis=-1, keepdims=True)
    y = (xf - mu) * lax.rsqrt(var + EPS)
    return (y * g.astype(f32) + b.astype(f32)).astype(x.dtype)


def adaln(cvec, w, b):
    m = jax.nn.silu(cvec) @ w + b
    return jnp.split(m[:, None, :], N_MOD, axis=-1)


def pre(x, g, shift, scale):
    return rmsnorm(x, g) * (1 + scale) + shift


def post(x, y, g, gate):
    return x + gate * rmsnorm(y, g)


def axial_rope_tables(L):
    rows = L // GRID_W
    row = jnp.repeat(jnp.arange(rows), GRID_W).astype(f32)
    col = jnp.tile(jnp.arange(GRID_W), rows).astype(f32)
    inv = ROPE_BASE ** (-jnp.arange(ROPE_FREQS, dtype=f32) / ROPE_FREQS)
    ang = jnp.concatenate([row[:, None] * inv, col[:, None] * inv], axis=-1)
    return jnp.cos(ang), jnp.sin(ang)


def apply_rope(x, cos, sin):
    B, L, H, _ = x.shape
    xr = x.astype(f32).reshape(B, L, H, 2, 2, ROPE_FREQS)
    x1, x2 = xr[..., 0, :], xr[..., 1, :]
    c = cos.reshape(L, 2, ROPE_FREQS)[None, :, None]
    s = sin.reshape(L, 2, ROPE_FREQS)[None, :, None]
    out = jnp.stack([x1 * c - x2 * s, x2 * c + x1 * s], axis=-2)
    return out.reshape(x.shape).astype(x.dtype)


def pool_mixer(u, w_pool, pool_scale):
    B, L, _ = u.shape
    ug = u.reshape(B, L, N_POOL_GROUPS, POOL_GROUP_DIM)
    cs = jnp.pad(jnp.cumsum(ug.astype(f32), axis=1), ((0, 0), (1, 0), (0, 0), (0, 0)))
    t = jnp.arange(L)[:, None]
    half = jnp.asarray(np.array([w // 2 for w in POOL_WINDOWS], dtype=np.int32))[None, :]
    start = jnp.clip(t - half, 0, L)
    end = jnp.clip(t + half, 0, L)
    g_idx = jnp.arange(N_POOL_GROUPS)[None, :]
    win_sum = cs[:, end, g_idx] - cs[:, start, g_idx]
    mean = win_sum / (end - start).astype(f32)[None, :, :, None]
    pooled = (mean - ug.astype(f32)).astype(u.dtype)
    y = jnp.einsum('blgc,gcd->blgd', pooled, w_pool)
    return y.reshape(B, L, POOL_WIDTH) * pool_scale


def band_mask(nb, L):
    i = jnp.arange(ATTN_BLOCK)[None, :, None]
    j = jnp.arange(3 * ATTN_BLOCK)[None, None, :]
    n = jnp.arange(nb)[:, None, None]
    q_pos = n * ATTN_BLOCK + i
    k_pos = (n - 1) * ATTN_BLOCK + j
    return (jnp.abs(k_pos - q_pos) <= WINDOW) & (k_pos >= 0) & (k_pos < L)


def windowed_gqa(q, k, v, kc, vc, sink):
    B, L = q.shape[:2]
    nb = L // ATTN_BLOCK
    scale = HEAD_DIM ** -0.5
    qb = q.reshape(B, nb, ATTN_BLOCK, N_KV_HEADS, GQA_GROUP, HEAD_DIM)

    def band(t):
        tb = jnp.pad(t.reshape(B, nb, ATTN_BLOCK, N_KV_HEADS, HEAD_DIM), ((0, 0), (1, 1), (0, 0), (0, 0), (0, 0)))
        return jnp.concatenate([tb[:, :-2], tb[:, 1:-1], tb[:, 2:]], axis=2)

    kb, vb = band(k), band(v)
    s_loc = jnp.einsum('bnqkgd,bnskd->bnkgqs', qb, kb, preferred_element_type=f32) * scale
    s_loc = jnp.where(band_mask(nb, L)[None, :, None, None], s_loc, NEG_INF)
    s_ctx = jnp.einsum('bnqkgd,bskd->bnkgqs', qb, kc, preferred_element_type=f32) * scale
    s_sink = jnp.broadcast_to(sink.astype(f32).reshape(N_KV_HEADS, GQA_GROUP, 1, 1), s_loc.shape[:-1] + (1,))
    p = jax.nn.softmax(jnp.concatenate([s_loc, s_ctx, s_sink], axis=-1), axis=-1)
    n_loc = 3 * ATTN_BLOCK
    p_loc = p[..., :n_loc].astype(v.dtype)
    p_ctx = p[..., n_loc:-1].astype(v.dtype)
    o = jnp.einsum('bnkgqs,bnskd->bnqkgd', p_loc, vb) + jnp.einsum('bnkgqs,bskd->bnqkgd', p_ctx, vc)
    return o.reshape(B, L, ATTN_WIDTH)


def context_gqa(qc, kc, vc, sink):
    B, C = qc.shape[:2]
    qg = qc.reshape(B, C, N_KV_HEADS, GQA_GROUP, HEAD_DIM)
    s = jnp.einsum('bqkgd,bskd->bkgqs', qg, kc, preferred_element_type=f32) * HEAD_DIM ** -0.5
    s_sink = jnp.broadcast_to(sink.astype(f32).reshape(N_KV_HEADS, GQA_GROUP, 1, 1), s.shape[:-1] + (1,))
    p = jax.nn.softmax(jnp.concatenate([s, s_sink], axis=-1), axis=-1)
    o = jnp.einsum('bkgqs,bskd->bqkgd', p[..., :-1].astype(vc.dtype), vc)
    return o.reshape(B, C, ATTN_WIDTH)


def even_mixer(h, hc, w_in, w_pool, pool_scale, sink, w_out, cos, sin, ctx_out):
    B, L, _ = h.shape
    C = hc.shape[1]
    z = h @ w_in
    u = z[..., :POOL_WIDTH]
    q = apply_rope(z[..., POOL_WIDTH:Q_END].reshape(B, L, N_Q_HEADS, HEAD_DIM), cos, sin)
    k = apply_rope(z[..., Q_END:Q_END + KV_WIDTH].reshape(B, L, N_KV_HEADS, HEAD_DIM), cos, sin)
    v = z[..., Q_END + KV_WIDTH:].reshape(B, L, N_KV_HEADS, HEAD_DIM)
    zc = hc @ w_in[:, Q_END:]
    kc = zc[..., :KV_WIDTH].reshape(B, C, N_KV_HEADS, HEAD_DIM)
    vc = zc[..., KV_WIDTH:].reshape(B, C, N_KV_HEADS, HEAD_DIM)
    y = jnp.concatenate([pool_mixer(u, w_pool, pool_scale), windowed_gqa(q, k, v, kc, vc, sink)], axis=-1) @ w_out
    yc = None
    if ctx_out:
        zq = hc @ w_in[:, :Q_END]
        qc = zq[..., POOL_WIDTH:].reshape(B, C, N_Q_HEADS, HEAD_DIM)
        yc = jnp.concatenate([pool_mixer(zq[..., :POOL_WIDTH], w_pool, pool_scale),
                              context_gqa(qc, kc, vc, sink)], axis=-1) @ w_out
    return y, yc


def sgu_mixer(h, w_in, ln_g, ln_b, w_s, b_s, w_out):
    B, L, _ = h.shape
    z = jax.nn.gelu(h @ w_in)
    u, v = jnp.split(z, 2, axis=-1)
    v = layernorm(v, ln_g, ln_b)
    vb = v.reshape(B, L // CHUNK, CHUNK, N_SGU_GROUPS, SGU_GROUP_DIM)
    s = jnp.einsum('gpq,bnqgc->bnpgc', w_s, vb) + b_s.T[:, :, None]
    return (u * s.reshape(B, L, SGU_WIDTH)) @ w_out


def conv_ffn(h, w_up, conv_w, conv_b, w_down):
    hu = h @ w_up
    hcv = lax.conv_general_dilated(hu, conv_w[:, None, :], window_strides=(1,), padding='SAME',
                                   dimension_numbers=('NWC', 'WIO', 'NWC'),
                                   feature_group_count=hu.shape[-1]) + conv_b
    gate, up = jnp.split(hcv, 2, axis=-1)
    return (jax.nn.silu(gate) * up) @ w_down


def setup_inputs(seed: int = 0) -> dict:
    key = jax.random.key(seed)
    ks = jax.random.split(key, 32)
    nrm = jax.random.normal
    D = D_MODEL
    return {
        "x": nrm(ks[0], (BATCH, SEQ, D), f32),
        "c": nrm(ks[1], (BATCH, D), f32),
        "ctx": nrm(ks[2], (BATCH, CTX_LEN, D), f32),
        "c_ctx": nrm(ks[3], (D,), f32),
        "w_ada": nrm(ks[4], (DEPTH, D, N_MOD * D), f32) * (0.5 * D ** -0.5),
        "b_ada": nrm(ks[5], (DEPTH, N_MOD * D), f32) * 0.01,
        "g_mix_pre": 1.0 + 0.1 * nrm(ks[6], (DEPTH, D), f32),
        "g_mix_post": 1.0 + 0.1 * nrm(ks[7], (DEPTH, D), f32),
        "g_ffn_pre": 1.0 + 0.1 * nrm(ks[8], (DEPTH, D), f32),
        "g_ffn_post": 1.0 + 0.1 * nrm(ks[9], (DEPTH, D), f32),
        "w_in_even": nrm(ks[10], (N_EVEN, D, IN_EVEN), f32) * D ** -0.5,
        "w_pool": nrm(ks[11], (N_EVEN, N_POOL_GROUPS, POOL_GROUP_DIM, POOL_GROUP_DIM), f32) * POOL_GROUP_DIM ** -0.5,
        "pool_scale": 1.0 + 0.1 * nrm(ks[12], (N_EVEN, POOL_WIDTH), f32),
        "attn_sink": 0.5 * nrm(ks[13], (N_EVEN, N_Q_HEADS), f32),
        "w_out_even": nrm(ks[14], (N_EVEN, MIX_EVEN, D), f32) * MIX_EVEN ** -0.5,
        "w_in_odd": nrm(ks[15], (N_ODD, D, 2 * SGU_WIDTH), f32) * D ** -0.5,
        "sgu_ln_g": 1.0 + 0.1 * nrm(ks[16], (N_ODD, SGU_WIDTH), f32),
        "sgu_ln_b": 0.01 * nrm(ks[17], (N_ODD, SGU_WIDTH), f32),
        "sgu_w": nrm(ks[18], (N_ODD, N_SGU_GROUPS, CHUNK, CHUNK), f32) * CHUNK ** -0.5,
        "sgu_b": 1.0 + 0.1 * nrm(ks[19], (N_ODD, N_SGU_GROUPS, CHUNK), f32),
        "w_out_odd": nrm(ks[20], (N_ODD, SGU_WIDTH, D), f32) * SGU_WIDTH ** -0.5,
        "w_ffn_up": nrm(ks[21], (DEPTH, D, 2 * D_FF), f32) * D ** -0.5,
        "ffn_conv_w": nrm(ks[22], (DEPTH, CONV_W, 2 * D_FF), f32) * CONV_W ** -0.5,
        "ffn_conv_b": 0.01 * nrm(ks[23], (DEPTH, 2 * D_FF), f32),
        "w_ffn_down": nrm(ks[24], (DEPTH, D_FF, D), f32) * D_FF ** -0.5,
    }


def reference(x, c, ctx, c_ctx, w_ada, b_ada, g_mix_pre, g_mix_post, g_ffn_pre, g_ffn_post,
              w_in_even, w_pool, pool_scale, attn_sink, w_out_even,
              w_in_odd, sgu_ln_g, sgu_ln_b, sgu_w, sgu_b, w_out_odd,
              w_ffn_up, ffn_conv_w, ffn_conv_b, w_ffn_down):
    L = x.shape[1]
    cos, sin = axial_rope_tables(L)
    xc = ctx
    for i in range(DEPTH):
        advance_ctx = any(j % 2 == 0 for j in range(i + 1, DEPTH))
        sh_m, sc_m, gt_m, sh_f, sc_f, gt_f = adaln(c, w_ada[i], b_ada[i])
        if i % 2 == 0 or advance_ctx:
            csh_m, csc_m, cgt_m, csh_f, csc_f, cgt_f = adaln(c_ctx[None, :], w_ada[i], b_ada[i])
        h = pre(x, g_mix_pre[i], sh_m, sc_m)
        if i % 2 == 0:
            e = i // 2
            hc = pre(xc, g_mix_pre[i], csh_m, csc_m)
            y, yc = even_mixer(h, hc, w_in_even[e], w_pool[e], pool_scale[e], attn_sink[e], w_out_even[e],
                               cos, sin, advance_ctx)
        else:
            o = i // 2
            y = sgu_mixer(h, w_in_odd[o], sgu_ln_g[o], sgu_ln_b[o], sgu_w[o], sgu_b[o], w_out_odd[o])
            yc = None
            if advance_ctx:
                yc = sgu_mixer(pre(xc, g_mix_pre[i], csh_m, csc_m), w_in_odd[o], sgu_ln_g[o], sgu_ln_b[o],
                               sgu_w[o], sgu_b[o], w_out_odd[o])
        x = post(x, y, g_mix_post[i], gt_m)
        x = post(x, conv_ffn(pre(x, g_ffn_pre[i], sh_f, sc_f), w_ffn_up[i], ffn_conv_w[i], ffn_conv_b[i],
                             w_ffn_down[i]), g_ffn_post[i], gt_f)
        if advance_ctx:
            xc = post(xc, yc, g_mix_post[i], cgt_m)
            xc = post(xc, conv_ffn(pre(xc, g_ffn_pre[i], csh_f, csc_f), w_ffn_up[i], ffn_conv_w[i],
                                   ffn_conv_b[i], w_ffn_down[i]), g_ffn_post[i], cgt_f)
    return x
```

```python
import jax
import jax.numpy as jnp
from jax import lax
from jax.experimental import pallas as pl
from jax.experimental.pallas import tpu as pltpu

D_MODEL = 1024
BATCH = 4
SEQ = 4096
GRID_W = 64
CTX_LEN = 256
EPS = 1e-6
NEG_INF = -1e30
N_MOD = 6
POOL_WINDOWS = (2, 4, 8, 16)
POOL_GROUP_DIM = 128
POOL_WIDTH = 512
HEAD_DIM = 64
N_KV_HEADS = 2
ATTN_WIDTH = 512
KV_WIDTH = 128
ATTN_BLOCK = 128
ROPE_BASE = 10000.0
ROPE_FREQS = 16
Q_END = POOL_WIDTH + ATTN_WIDTH
CHUNK = 128
N_SGU_GROUPS = 8
D_FF = 2816

LANE = 128
HALO = 16
VMEM_LIMIT = 56 * 1024 * 1024

N_TOK = BATCH * SEQ
TM_IN = 512
TQ = 512
TM_FFN = 512
TM_ODD = 512
FF_CHUNK = 256
KV_DUP = 2 * KV_WIDTH

f32 = jnp.float32
bf16 = jnp.bfloat16

SH_M, SC_M, GT_M, SH_F, SC_F, GT_F = range(N_MOD)
CTX_ROW = BATCH


def _rms(x, g):
    return x * lax.rsqrt(jnp.mean(x * x, axis=-1, keepdims=True) + EPS) * g


def _pre(x, g, shift, scale):
    return _rms(x, g) * (1.0 + scale) + shift


def _mod_row(mods_ref, slot, row):
    return mods_ref[slot, pl.ds(row, 1), :]


def _const_spec(shape):
    zeros = (0,) * len(shape)
    return pl.BlockSpec(shape, lambda *_: zeros, pipeline_mode=pl.Buffered(1))


def _mods_spec(layer):
    return pl.BlockSpec((None, N_MOD, 8, D_MODEL), lambda *_: (layer, 0, 0, 0))


def _vec_spec(width):
    return pl.BlockSpec((1, width), lambda *_: (0, 0))


def _params(n_axes=1):
    return pltpu.CompilerParams(dimension_semantics=("arbitrary",) * n_axes,
                                vmem_limit_bytes=VMEM_LIMIT)


def _adaln_kernel(c_ref, w_ref, b_ref, o_ref):
    s = jax.nn.silu(c_ref[...]).astype(bf16)
    o_ref[...] = jnp.dot(s, w_ref[...].astype(bf16), preferred_element_type=f32) + b_ref[...]


def _adaln(cvec, w_ada, b_ada):
    depth = w_ada.shape[0]
    return pl.pallas_call(
        _adaln_kernel,
        out_shape=jax.ShapeDtypeStruct((depth, N_MOD, 8, D_MODEL), f32),
        grid=(depth, N_MOD),
        in_specs=[
            pl.BlockSpec((8, D_MODEL), lambda l, j: (0, 0)),
            pl.BlockSpec((None, D_MODEL, D_MODEL), lambda l, j: (l, 0, j)),
            pl.BlockSpec((None, None, 1, D_MODEL), lambda l, j: (l, j, 0, 0)),
        ],
        out_specs=pl.BlockSpec((None, None, 8, D_MODEL), lambda l, j: (l, j, 0, 0)),
        compiler_params=_params(2),
        name="adaln",
    )(cvec, w_ada, b_ada.reshape(depth, N_MOD, 1, D_MODEL))


def _ctx_kv_kernel(x_ref, mods_ref, g_ref, w_ref, k_ref, v_ref):
    h = _pre(x_ref[...], g_ref[...], mods_ref[SH_M, CTX_ROW:CTX_ROW + 1, :],
             mods_ref[SC_M, CTX_ROW:CTX_ROW + 1, :]).astype(bf16)
    z = jnp.dot(h, w_ref[...], preferred_element_type=f32)
    k_ref[...] = z[:, :KV_DUP].astype(bf16)
    v_ref[...] = z[:, KV_DUP:].astype(bf16)


def _ctx_kv(ctx2d, mods, g, w_kv):
    n = ctx2d.shape[0]
    return pl.pallas_call(
        _ctx_kv_kernel,
        out_shape=(jax.ShapeDtypeStruct((n, KV_DUP), bf16),
                   jax.ShapeDtypeStruct((n, KV_DUP), bf16)),
        grid=(1,),
        in_specs=[
            pl.BlockSpec((n, D_MODEL), lambda i: (0, 0)),
            _mods_spec(0),
            _vec_spec(D_MODEL),
            pl.BlockSpec(w_kv.shape, lambda i: (0, 0)),
        ],
        out_specs=(pl.BlockSpec((n, KV_DUP), lambda i: (0, 0)),
                   pl.BlockSpec((n, KV_DUP), lambda i: (0, 0))),
        compiler_params=_params(1),
        name="ctx_kv",
    )(ctx2d, mods, g, w_kv)


def _halo_specs(tm, width):
    per = tm // HALO
    last = N_TOK // HALO - 1
    return [
        pl.BlockSpec((tm, width), lambda i: (i, 0)),
        pl.BlockSpec((HALO, width), lambda i: (jnp.maximum(i * per - 1, 0), 0)),
        pl.BlockSpec((HALO, width), lambda i: (jnp.minimum((i + 1) * per, last), 0)),
    ]


def _seq_tile(tm):
    tps = SEQ // tm
    i = pl.program_id(0)
    return i // tps, i % tps, tps


def _normed_with_halo(xm_ref, xp_ref, xn_ref, g, shift, scale, tm):
    _, ti, tps = _seq_tile(tm)
    hp = _pre(xp_ref[...], g, shift, scale)
    hn = _pre(xn_ref[...], g, shift, scale)
    hp = jnp.where(ti > 0, hp, 0.0)
    hn = jnp.where(ti < tps - 1, hn, 0.0)
    hm = _pre(xm_ref[...], g, shift, scale)
    return jnp.concatenate([hp, hm, hn], axis=0).astype(bf16)


def _rope(z, cos, sin_signed, lane):
    fwd = pltpu.roll(z, LANE - ROPE_FREQS, axis=1)
    bwd = pltpu.roll(z, ROPE_FREQS, axis=1)
    partner = jnp.where((lane % (2 * ROPE_FREQS)) < ROPE_FREQS, fwd, bwd)
    return z * cos + partner * sin_signed


def _even_in_kernel(xm_ref, xp_ref, xn_ref, mods_ref, g_ref, w_ref, cos_ref, sin_ref,
                    wpool_ref, pscale_ref, yp_ref, q_ref, k_ref, v_ref):
    tm = TM_IN
    b, ti, _ = _seq_tile(tm)
    shift = _mod_row(mods_ref, SH_M, b)
    scale = _mod_row(mods_ref, SC_M, b)
    h = _normed_with_halo(xm_ref, xp_ref, xn_ref, g_ref[...], shift, scale, tm)
    z = jnp.dot(h, w_ref[...], preferred_element_type=f32)

    zm = z[HALO:HALO + tm]
    cos = cos_ref[...]
    sin = sin_ref[...]
    lane = lax.broadcasted_iota(jnp.int32, (tm, LANE), 1)
    for j in range(ATTN_WIDTH // LANE):
        c0 = POOL_WIDTH + j * LANE
        q_ref[:, j * LANE:(j + 1) * LANE] = (
            _rope(zm[:, c0:c0 + LANE], cos, sin, lane) * (HEAD_DIM ** -0.5)).astype(bf16)
    for j in range(KV_DUP // LANE):
        c0 = Q_END + j * LANE
        k_ref[:, j * LANE:(j + 1) * LANE] = _rope(zm[:, c0:c0 + LANE], cos, sin, lane).astype(bf16)
    v_ref[...] = zm[:, Q_END + KV_DUP:].astype(bf16)

    rows = tm + 2 * HALO
    pos = lax.broadcasted_iota(jnp.int32, (tm, POOL_GROUP_DIM), 0) + ti * tm
    for gi, win in enumerate(POOL_WINDOWS):
        lanes = slice(gi * POOL_GROUP_DIM, (gi + 1) * POOL_GROUP_DIM)
        u = z[:, lanes]
        s = u + pltpu.roll(u, 1, axis=0)
        step = 1
        while 2 * step < win:
            s = pltpu.roll(s, step, axis=0) + pltpu.roll(s, rows - step, axis=0)
            step *= 2
        half = win // 2
        cnt = (jnp.minimum(pos + half, SEQ) - jnp.maximum(pos - half, 0)).astype(f32)
        pooled = (s[HALO:HALO + tm] / cnt - u[HALO:HALO + tm]).astype(bf16)
        y = jnp.dot(pooled, wpool_ref[gi], preferred_element_type=f32)
        yp_ref[:, lanes] = (y * pscale_ref[:, lanes]).astype(bf16)


def _even_in(x2d, mods, g, w1, cos, sin, w_pool, pool_scale):
    tm = TM_IN
    tps = SEQ // tm
    tok = lambda w: pl.BlockSpec((tm, w), lambda i: (i, 0))
    return pl.pallas_call(
        _even_in_kernel,
        out_shape=(jax.ShapeDtypeStruct((N_TOK, POOL_WIDTH), bf16),
                   jax.ShapeDtypeStruct((N_TOK, ATTN_WIDTH), bf16),
                   jax.ShapeDtypeStruct((N_TOK, KV_DUP), bf16),
                   jax.ShapeDtypeStruct((N_TOK, KV_DUP), bf16)),
        grid=(N_TOK // tm,),
        in_specs=_halo_specs(tm, D_MODEL) + [
            _mods_spec(0),
            _vec_spec(D_MODEL),
            _const_spec(w1.shape),
            pl.BlockSpec((tm, LANE), lambda i: (i % tps, 0)),
            pl.BlockSpec((tm, LANE), lambda i: (i % tps, 0)),
            _const_spec(w_pool.shape),
            _vec_spec(POOL_WIDTH),
        ],
        out_specs=(tok(POOL_WIDTH), tok(ATTN_WIDTH), tok(KV_DUP), tok(KV_DUP)),
        compiler_params=_params(1),
        name="even_in",
    )(x2d, x2d, x2d, mods, g, w1, cos, sin, w_pool, pool_scale)


def _split_halves(blk):
    low = lax.broadcasted_iota(jnp.int32, blk.shape, 1) < HEAD_DIM
    zero = jnp.zeros_like(blk)
    return jnp.where(low, blk, zero), jnp.where(low, zero, blk)


def _even_attn_kernel(sink_ref, q_ref, km_ref, kp_ref, kn_ref, vm_ref, vp_ref, vn_ref,
                      kc_ref, vc_ref, yp_ref, x_ref, mods_ref, g_ref, wout_ref, o_ref, mix_ref):
    b, ti, tps = _seq_tile(TQ)
    nblk = TQ // ATTN_BLOCK

    def row_blocks(main_ref, prev_ref, next_ref, ctx_ref, c0):
        blocks = [prev_ref[:, c0:c0 + LANE]]
        blocks += [main_ref[j * ATTN_BLOCK:(j + 1) * ATTN_BLOCK, c0:c0 + LANE] for j in range(nblk)]
        blocks += [next_ref[:, c0:c0 + LANE], ctx_ref[:, c0:c0 + LANE]]
        return [_split_halves(blk) for blk in blocks]

    ri = lax.broadcasted_iota(jnp.int32, (ATTN_BLOCK, ATTN_BLOCK), 0)
    ci = lax.broadcasted_iota(jnp.int32, (ATTN_BLOCK, ATTN_BLOCK), 1)
    lower_ok = ci >= ri
    upper_ok = ci <= ri
    first_ok = jnp.logical_and(lower_ok, ti > 0)
    last_ok = jnp.logical_and(upper_ok, ti < tps - 1)
    contract_last = (((1,), (1,)), ((), ()))

    mix_ref[:, :POOL_WIDTH] = yp_ref[...]
    for g in range(N_KV_HEADS):
        kblocks = row_blocks(km_ref, kp_ref, kn_ref, kc_ref, g * LANE)
        vblocks = row_blocks(vm_ref, vp_ref, vn_ref, vc_ref, g * LANE)
        for j in range(nblk):
            sel = [j, j + 1, j + 2, nblk + 2]
            keys = [jnp.concatenate([kblocks[s][par] for s in sel], axis=0) for par in range(2)]
            vals = [jnp.concatenate([vblocks[s][par] for s in sel], axis=0) for par in range(2)]
            rows = slice(j * ATTN_BLOCK, (j + 1) * ATTN_BLOCK)
            qg = jnp.concatenate([q_ref[rows, (2 * g) * LANE:(2 * g + 1) * LANE],
                                  q_ref[rows, (2 * g + 1) * LANE:(2 * g + 2) * LANE]], axis=0)
            m_prev = first_ok if j == 0 else lower_ok
            m_next = last_ok if j == nblk - 1 else upper_ok
            probs = []
            inv_l = []
            for par in range(2):
                s_all = lax.dot_general(qg, keys[par], contract_last, preferred_element_type=f32)
                p_rows = []
                l_rows = []
                for pr in range(2):
                    sink = sink_ref[4 * g + 2 * pr + par]
                    s = s_all[pr * ATTN_BLOCK:(pr + 1) * ATTN_BLOCK]
                    s = jnp.concatenate([
                        jnp.where(m_prev, s[:, :ATTN_BLOCK], NEG_INF),
                        s[:, ATTN_BLOCK:2 * ATTN_BLOCK],
                        jnp.where(m_next, s[:, 2 * ATTN_BLOCK:3 * ATTN_BLOCK], NEG_INF),
                        s[:, 3 * ATTN_BLOCK:]], axis=1)
                    m = jnp.maximum(jnp.max(s, axis=-1, keepdims=True), sink)
                    p = jnp.exp(s - m)
                    l = jnp.sum(p, axis=-1, keepdims=True) + jnp.exp(sink - m)
                    p_rows.append(p.astype(bf16))
                    l_rows.append(1.0 / l)
                probs.append(jnp.concatenate(p_rows, axis=0))
                inv_l.append(jnp.concatenate(l_rows, axis=0))
            o = (jnp.dot(probs[0], vals[0], preferred_element_type=f32)
                 + jnp.dot(probs[1], vals[1], preferred_element_type=f32))
            low = lax.broadcasted_iota(jnp.int32, o.shape, 1) < HEAD_DIM
            o = o * jnp.where(low, inv_l[0], inv_l[1])
            for pr in range(2):
                c0 = POOL_WIDTH + (2 * g + pr) * LANE
                mix_ref[rows, c0:c0 + LANE] = o[pr * ATTN_BLOCK:(pr + 1) * ATTN_BLOCK].astype(bf16)

    y = jnp.dot(mix_ref[...], wout_ref[...], preferred_element_type=f32)
    o_ref[...] = x_ref[...] + _mod_row(mods_ref, GT_M, b) * _rms(y, g_ref[...])


def _even_attn(sink, q, k, v, kc, vc, yp, x2d, mods, g_post, w_out):
    per = TQ // ATTN_BLOCK
    last = N_TOK // ATTN_BLOCK - 1
    tps = SEQ // TQ
    band = [
        pl.BlockSpec((TQ, KV_DUP), lambda i: (i, 0)),
        pl.BlockSpec((ATTN_BLOCK, KV_DUP), lambda i: (jnp.maximum(i * per - 1, 0), 0)),
        pl.BlockSpec((ATTN_BLOCK, KV_DUP), lambda i: (jnp.minimum((i + 1) * per, last), 0)),
    ]
    ctx_spec = pl.BlockSpec((CTX_LEN, KV_DUP), lambda i: (i // tps, 0))
    return pl.pallas_call(
        _even_attn_kernel,
        out_shape=jax.ShapeDtypeStruct((N_TOK, D_MODEL), f32),
        grid=(N_TOK // TQ,),
        in_specs=[pl.BlockSpec(memory_space=pltpu.SMEM),
                  pl.BlockSpec((TQ, ATTN_WIDTH), lambda i: (i, 0))]
                 + band + band + [
            ctx_spec, ctx_spec,
            pl.BlockSpec((TQ, POOL_WIDTH), lambda i: (i, 0)),
            pl.BlockSpec((TQ, D_MODEL), lambda i: (i, 0)),
            _mods_spec(0),
            _vec_spec(D_MODEL),
            _const_spec(w_out.shape),
        ],
        out_specs=pl.BlockSpec((TQ, D_MODEL), lambda i: (i, 0)),
        scratch_shapes=[pltpu.VMEM((TQ, POOL_WIDTH + ATTN_WIDTH), bf16)],
        compiler_params=_params(1),
        name="even_attn",
    )(sink, q, k, k, k, v, v, v, kc, vc, yp, x2d, mods, g_post, w_out)


def _conv_ffn_kernel(xm_ref, xp_ref, xn_ref, mods_ref, gpre_ref, gpost_ref, wup_ref, cw_ref, cb_ref,
                     wdn_ref, o_ref):
    tm = TM_FFN
    rows = tm + 2 * HALO
    b, _, _ = _seq_tile(tm)
    h = _normed_with_halo(xm_ref, xp_ref, xn_ref, gpre_ref[...], _mod_row(mods_ref, SH_F, b),
                          _mod_row(mods_ref, SC_F, b), tm)

    def conv(hu, c0):
        w = cw_ref[:, c0:c0 + FF_CHUNK]
        y = (pltpu.roll(hu, 1, axis=0) * w[0:1] + hu * w[1:2]
             + pltpu.roll(hu, rows - 1, axis=0) * w[2:3])
        return y[HALO:HALO + tm] + cb_ref[:, c0:c0 + FF_CHUNK]

    acc = jnp.zeros((tm, D_MODEL), f32)
    for c in range(D_FF // FF_CHUNK):
        g0 = c * FF_CHUNK
        u0 = D_FF + c * FF_CHUNK
        gate = conv(jnp.dot(h, wup_ref[:, g0:g0 + FF_CHUNK], preferred_element_type=f32), g0)
        up = conv(jnp.dot(h, wup_ref[:, u0:u0 + FF_CHUNK], preferred_element_type=f32), u0)
        act = (jax.nn.silu(gate) * up).astype(bf16)
        acc = acc + jnp.dot(act, wdn_ref[g0:g0 + FF_CHUNK, :], preferred_element_type=f32)

    o_ref[...] = xm_ref[...] + _mod_row(mods_ref, GT_F, b) * _rms(acc, gpost_ref[...])


def _conv_ffn(x2d, mods, layer, g_pre, g_post, w_up, conv_w, conv_b, w_down):
    tm = TM_FFN
    return pl.pallas_call(
        _conv_ffn_kernel,
        out_shape=jax.ShapeDtypeStruct((N_TOK, D_MODEL), f32),
        grid=(N_TOK // tm,),
        in_specs=_halo_specs(tm, D_MODEL) + [
            _mods_spec(layer),
            _vec_spec(D_MODEL),
            _vec_spec(D_MODEL),
            _const_spec(w_up.shape),
            pl.BlockSpec(conv_w.shape, lambda i: (0, 0)),
            pl.BlockSpec(conv_b.shape, lambda i: (0, 0)),
            _const_spec(w_down.shape),
        ],
        out_specs=pl.BlockSpec((tm, D_MODEL), lambda i: (i, 0)),
        compiler_params=_params(1),
        name="conv_ffn",
    )(x2d, x2d, x2d, mods, g_pre, g_post, w_up, conv_w, conv_b, w_down)


def _odd_mixer_kernel(x_ref, mods_ref, gpre_ref, gpost_ref, win_ref, lng_ref, lnb_ref, ws_ref, bs_ref,
                      wout_ref, o_ref, gated_ref):
    tm = TM_ODD
    nch = tm // CHUNK
    gw = D_MODEL // N_SGU_GROUPS
    b, _, _ = _seq_tile(tm)
    x = x_ref[...]
    h = _pre(x, gpre_ref[...], _mod_row(mods_ref, SH_M, b), _mod_row(mods_ref, SC_M, b)).astype(bf16)
    z = jax.nn.gelu(jnp.dot(h, win_ref[...], preferred_element_type=f32))
    u = z[:, :D_MODEL]
    v = z[:, D_MODEL:]
    mu = jnp.mean(v, axis=-1, keepdims=True)
    vc = v - mu
    var = jnp.mean(vc * vc, axis=-1, keepdims=True)
    vn = (vc * lax.rsqrt(var + EPS) * lng_ref[...] + lnb_ref[...]).astype(bf16)
    for g in range(N_SGU_GROUPS):
        lanes = slice(g * gw, (g + 1) * gw)
        vg = jnp.concatenate([vn[n * CHUNK:(n + 1) * CHUNK, lanes] for n in range(nch)], axis=1)
        s = jnp.dot(ws_ref[g], vg, preferred_element_type=f32)
        for n in range(nch):
            rows = slice(n * CHUNK, (n + 1) * CHUNK)
            gated_ref[rows, lanes] = (u[rows, lanes] * (s[:, n * gw:(n + 1) * gw] + bs_ref[g])).astype(bf16)
    y = jnp.dot(gated_ref[...], wout_ref[...], preferred_element_type=f32)
    o_ref[...] = x + _mod_row(mods_ref, GT_M, b) * _rms(y, gpost_ref[...])


def _odd_mixer(x2d, mods, g_pre, g_post, w_in, ln_g, ln_b, w_s, b_s, w_out):
    tm = TM_ODD
    return pl.pallas_call(
        _odd_mixer_kernel,
        out_shape=jax.ShapeDtypeStruct((N_TOK, D_MODEL), f32),
        grid=(N_TOK // tm,),
        in_specs=[
            pl.BlockSpec((tm, D_MODEL), lambda i: (i, 0)),
            _mods_spec(1),
            _vec_spec(D_MODEL), _vec_spec(D_MODEL),
            _const_spec(w_in.shape),
            _vec_spec(D_MODEL), _vec_spec(D_MODEL),
            _const_spec(w_s.shape),
            _const_spec(b_s.shape),
            _const_spec(w_out.shape),
        ],
        out_specs=pl.BlockSpec((tm, D_MODEL), lambda i: (i, 0)),
        scratch_shapes=[pltpu.VMEM((tm, D_MODEL), bf16)],
        compiler_params=_params(1),
        name="odd_mixer",
    )(x2d, mods, g_pre, g_post, w_in, ln_g, ln_b, w_s, b_s, w_out)


def _rope_tables():
    rows = SEQ // GRID_W
    row = jnp.repeat(jnp.arange(rows), GRID_W).astype(f32)
    col = jnp.tile(jnp.arange(GRID_W), rows).astype(f32)
    inv = ROPE_BASE ** (-jnp.arange(ROPE_FREQS, dtype=f32) / ROPE_FREQS)
    ar = row[:, None] * inv
    ac = col[:, None] * inv
    cos = jnp.concatenate([jnp.cos(ar), jnp.cos(ar), jnp.cos(ac), jnp.cos(ac)] * 2, axis=1)
    sin = jnp.concatenate([-jnp.sin(ar), jnp.sin(ar), -jnp.sin(ac), jnp.sin(ac)] * 2, axis=1)
    return cos, sin


def kernel(x, c, ctx, c_ctx, w_ada, b_ada, g_mix_pre, g_mix_post, g_ffn_pre, g_ffn_post, w_in_even, w_pool,
           pool_scale, attn_sink, w_out_even, w_in_odd, sgu_ln_g, sgu_ln_b, sgu_w, sgu_b, w_out_odd,
           w_ffn_up, ffn_conv_w, ffn_conv_b, w_ffn_down):
    assert x.shape == (BATCH, SEQ, D_MODEL) and ctx.shape == (BATCH, CTX_LEN, D_MODEL)
    x2d = x.reshape(N_TOK, D_MODEL)
    ctx2d = ctx.reshape(BATCH * CTX_LEN, D_MODEL)
    row = lambda a: a[None, :]

    cvec = jnp.concatenate([c, c_ctx[None, :], jnp.zeros((8 - BATCH - 1, D_MODEL), f32)], axis=0)
    mods = _adaln(cvec, w_ada, b_ada)

    w = w_in_even[0]
    dup = [w[:, Q_END + i * HEAD_DIM:Q_END + (i + 1) * HEAD_DIM] for i in range(2 * N_KV_HEADS)]
    w1 = jnp.concatenate([w[:, :Q_END]] + [d for d in dup for _ in range(2)], axis=1).astype(bf16)
    cos, sin = _rope_tables()
    kc, vc = _ctx_kv(ctx2d, mods, row(g_mix_pre[0]), w1[:, Q_END:])
    yp, q, k, v = _even_in(x2d, mods, row(g_mix_pre[0]), w1, cos, sin, w_pool[0].astype(bf16),
                           row(pool_scale[0]))
    x2d = _even_attn(attn_sink[0], q, k, v, kc, vc, yp, x2d, mods, row(g_mix_post[0]),
                     w_out_even[0].astype(bf16))
    x2d = _conv_ffn(x2d, mods, 0, row(g_ffn_pre[0]), row(g_ffn_post[0]), w_ffn_up[0].astype(bf16),
                    ffn_conv_w[0], row(ffn_conv_b[0]), w_ffn_down[0].astype(bf16))

    b_s = jnp.broadcast_to(sgu_b[0][:, :, None], (N_SGU_GROUPS, CHUNK, CHUNK))
    x2d = _odd_mixer(x2d, mods, row(g_mix_pre[1]), row(g_mix_post[1]), w_in_odd[0].astype(bf16),
                     row(sgu_ln_g[0]), row(sgu_ln_b[0]), sgu_w[0].astype(bf16), b_s,
                     w_out_odd[0].astype(bf16))
    x2d = _conv_ffn(x2d, mods, 1, row(g_ffn_pre[1]), row(g_ffn_post[1]), w_ffn_up[1].astype(bf16),
                    ffn_conv_w[1], row(ffn_conv_b[1]), w_ffn_down[1].astype(bf16))
    return x2d.reshape(BATCH, SEQ, D_MODEL)
```

```python
import jax
import jax.numpy as jnp
from jax import lax
from jax.experimental import pallas as pl
from jax.experimental.pallas import tpu as pltpu

D_MODEL = 1024
BATCH = 4
SEQ = 4096
GRID_W = 64
CTX_LEN = 256
EPS = 1e-6
NEG_INF = -1e30
N_MOD = 6
POOL_WINDOWS = (2, 4, 8, 16)
POOL_GROUP_DIM = 128
POOL_WIDTH = 512
HEAD_DIM = 64
N_KV_HEADS = 2
ATTN_WIDTH = 512
KV_WIDTH = 128
ATTN_BLOCK = 128
ROPE_BASE = 10000.0
ROPE_FREQS = 16
Q_END = POOL_WIDTH + ATTN_WIDTH
CHUNK = 128
N_SGU_GROUPS = 8
D_FF = 2816

LANE = 128
SUBLANE = 8
HALO = 16
VMEM_LIMIT = 56 * 1024 * 1024

N_TOK = BATCH * SEQ
TM_IN = 512
TQ = 512
TM_FFN = 512
TM_ODD = 512
FF_CHUNK = 256
SLAB_PITCH = TM_FFN // SUBLANE + SUBLANE
KV_DUP = 2 * KV_WIDTH

f32 = jnp.float32
bf16 = jnp.bfloat16

SH_M, SC_M, GT_M, SH_F, SC_F, GT_F = range(N_MOD)
CTX_ROW = BATCH


def _rms(x, g):
    return x * lax.rsqrt(jnp.mean(x * x, axis=-1, keepdims=True) + EPS) * g


def _pre(x, g, shift, scale):
    return _rms(x, g) * (1.0 + scale) + shift


def _mod_row(mods_ref, slot, row):
    return mods_ref[slot, pl.ds(row, 1), :]


def _const_spec(shape):
    zeros = (0,) * len(shape)
    return pl.BlockSpec(shape, lambda *_: zeros, pipeline_mode=pl.Buffered(1))


def _mods_spec(layer):
    return pl.BlockSpec((None, N_MOD, 8, D_MODEL), lambda *_: (layer, 0, 0, 0))


def _vec_spec(width):
    return pl.BlockSpec((1, width), lambda *_: (0, 0))


def _params(n_axes=1):
    return pltpu.CompilerParams(dimension_semantics=("arbitrary",) * n_axes,
                                vmem_limit_bytes=VMEM_LIMIT)


def _adaln_kernel(c_ref, w_ref, b_ref, o_ref):
    s = jax.nn.silu(c_ref[...]).astype(bf16)
    o_ref[...] = jnp.dot(s, w_ref[...].astype(bf16), preferred_element_type=f32) + b_ref[...]


def _adaln(cvec, w_ada, b_ada):
    depth = w_ada.shape[0]
    return pl.pallas_call(
        _adaln_kernel,
        out_shape=jax.ShapeDtypeStruct((depth, N_MOD, 8, D_MODEL), f32),
        grid=(depth, N_MOD),
        in_specs=[
            pl.BlockSpec((8, D_MODEL), lambda l, j: (0, 0)),
            pl.BlockSpec((None, D_MODEL, D_MODEL), lambda l, j: (l, 0, j)),
            pl.BlockSpec((None, None, 1, D_MODEL), lambda l, j: (l, j, 0, 0)),
        ],
        out_specs=pl.BlockSpec((None, None, 8, D_MODEL), lambda l, j: (l, j, 0, 0)),
        compiler_params=_params(2),
        name="adaln",
    )(cvec, w_ada, b_ada.reshape(depth, N_MOD, 1, D_MODEL))


def _ctx_kv_kernel(x_ref, mods_ref, g_ref, w_ref, k_ref, v_ref):
    h = _pre(x_ref[...], g_ref[...], mods_ref[SH_M, CTX_ROW:CTX_ROW + 1, :],
             mods_ref[SC_M, CTX_ROW:CTX_ROW + 1, :]).astype(bf16)
    z = jnp.dot(h, w_ref[...], preferred_element_type=f32)
    k_ref[...] = z[:, :KV_DUP].astype(bf16)
    v_ref[...] = z[:, KV_DUP:].astype(bf16)


def _ctx_kv(ctx2d, mods, g, w_kv):
    n = ctx2d.shape[0]
    return pl.pallas_call(
        _ctx_kv_kernel,
        out_shape=(jax.ShapeDtypeStruct((n, KV_DUP), bf16),
                   jax.ShapeDtypeStruct((n, KV_DUP), bf16)),
        grid=(1,),
        in_specs=[
            pl.BlockSpec((n, D_MODEL), lambda i: (0, 0)),
            _mods_spec(0),
            _vec_spec(D_MODEL),
            pl.BlockSpec(w_kv.shape, lambda i: (0, 0)),
        ],
        out_specs=(pl.BlockSpec((n, KV_DUP), lambda i: (0, 0)),
                   pl.BlockSpec((n, KV_DUP), lambda i: (0, 0))),
        compiler_params=_params(1),
        name="ctx_kv",
    )(ctx2d, mods, g, w_kv)


def _halo_specs(tm, width, halo=HALO):
    per = tm // halo
    last = N_TOK // halo - 1
    return [
        pl.BlockSpec((tm, width), lambda i: (i, 0)),
        pl.BlockSpec((halo, width), lambda i: (jnp.maximum(i * per - 1, 0), 0)),
        pl.BlockSpec((halo, width), lambda i: (jnp.minimum((i + 1) * per, last), 0)),
    ]


def _seq_tile(tm):
    tps = SEQ // tm
    i = pl.program_id(0)
    return i // tps, i % tps, tps


def _normed_with_halo(xm_ref, xp_ref, xn_ref, g, shift, scale, tm):
    _, ti, tps = _seq_tile(tm)
    hp = _pre(xp_ref[...], g, shift, scale)
    hn = _pre(xn_ref[...], g, shift, scale)
    hp = jnp.where(ti > 0, hp, 0.0)
    hn = jnp.where(ti < tps - 1, hn, 0.0)
    hm = _pre(xm_ref[...], g, shift, scale)
    return jnp.concatenate([hp, hm, hn], axis=0).astype(bf16)


def _rope(z, cos, sin_signed, lane):
    fwd = pltpu.roll(z, LANE - ROPE_FREQS, axis=1)
    bwd = pltpu.roll(z, ROPE_FREQS, axis=1)
    partner = jnp.where((lane % (2 * ROPE_FREQS)) < ROPE_FREQS, fwd, bwd)
    return z * cos + partner * sin_signed


def _even_in_kernel(xm_ref, xp_ref, xn_ref, mods_ref, g_ref, w_ref, cos_ref, sin_ref,
                    wpool_ref, pscale_ref, yp_ref, q_ref, k_ref, v_ref):
    tm = TM_IN
    b, ti, _ = _seq_tile(tm)
    shift = _mod_row(mods_ref, SH_M, b)
    scale = _mod_row(mods_ref, SC_M, b)
    h = _normed_with_halo(xm_ref, xp_ref, xn_ref, g_ref[...], shift, scale, tm)
    z = jnp.dot(h, w_ref[...], preferred_element_type=f32)

    zm = z[HALO:HALO + tm]
    cos = cos_ref[...]
    sin = sin_ref[...]
    lane = lax.broadcasted_iota(jnp.int32, (tm, LANE), 1)
    for j in range(ATTN_WIDTH // LANE):
        c0 = POOL_WIDTH + j * LANE
        q_ref[:, j * LANE:(j + 1) * LANE] = (
            _rope(zm[:, c0:c0 + LANE], cos, sin, lane) * (HEAD_DIM ** -0.5)).astype(bf16)
    for j in range(KV_DUP // LANE):
        c0 = Q_END + j * LANE
        k_ref[:, j * LANE:(j + 1) * LANE] = _rope(zm[:, c0:c0 + LANE], cos, sin, lane).astype(bf16)
    v_ref[...] = zm[:, Q_END + KV_DUP:].astype(bf16)

    rows = tm + 2 * HALO
    pos = lax.broadcasted_iota(jnp.int32, (tm, POOL_GROUP_DIM), 0) + ti * tm
    for gi, win in enumerate(POOL_WINDOWS):
        lanes = slice(gi * POOL_GROUP_DIM, (gi + 1) * POOL_GROUP_DIM)
        u = z[:, lanes]
        s = u + pltpu.roll(u, 1, axis=0)
        step = 1
        while 2 * step < win:
            s = pltpu.roll(s, step, axis=0) + pltpu.roll(s, rows - step, axis=0)
            step *= 2
        half = win // 2
        cnt = (jnp.minimum(pos + half, SEQ) - jnp.maximum(pos - half, 0)).astype(f32)
        pooled = (s[HALO:HALO + tm] / cnt - u[HALO:HALO + tm]).astype(bf16)
        y = jnp.dot(pooled, wpool_ref[gi], preferred_element_type=f32)
        yp_ref[:, lanes] = (y * pscale_ref[:, lanes]).astype(bf16)


def _even_in(x2d, mods, g, w1, cos, sin, w_pool, pool_scale):
    tm = TM_IN
    tps = SEQ // tm
    tok = lambda w: pl.BlockSpec((tm, w), lambda i: (i, 0))
    return pl.pallas_call(
        _even_in_kernel,
        out_shape=(jax.ShapeDtypeStruct((N_TOK, POOL_WIDTH), bf16),
                   jax.ShapeDtypeStruct((N_TOK, ATTN_WIDTH), bf16),
                   jax.ShapeDtypeStruct((N_TOK, KV_DUP), bf16),
                   jax.ShapeDtypeStruct((N_TOK, KV_DUP), bf16)),
        grid=(N_TOK // tm,),
        in_specs=_halo_specs(tm, D_MODEL) + [
            _mods_spec(0),
            _vec_spec(D_MODEL),
            _const_spec(w1.shape),
            pl.BlockSpec((tm, LANE), lambda i: (i % tps, 0)),
            pl.BlockSpec((tm, LANE), lambda i: (i % tps, 0)),
            _const_spec(w_pool.shape),
            _vec_spec(POOL_WIDTH),
        ],
        out_specs=(tok(POOL_WIDTH), tok(ATTN_WIDTH), tok(KV_DUP), tok(KV_DUP)),
        compiler_params=_params(1),
        name="even_in",
    )(x2d, x2d, x2d, mods, g, w1, cos, sin, w_pool, pool_scale)


def _split_halves(blk):
    low = lax.broadcasted_iota(jnp.int32, blk.shape, 1) < HEAD_DIM
    zero = jnp.zeros_like(blk)
    return jnp.where(low, blk, zero), jnp.where(low, zero, blk)


def _even_attn_kernel(sink_ref, q_ref, km_ref, kp_ref, kn_ref, vm_ref, vp_ref, vn_ref,
                      kc_ref, vc_ref, yp_ref, x_ref, mods_ref, g_ref, wout_ref, o_ref, mix_ref):
    b, ti, tps = _seq_tile(TQ)
    nblk = TQ // ATTN_BLOCK

    def row_blocks(main_ref, prev_ref, next_ref, ctx_ref, c0):
        blocks = [prev_ref[:, c0:c0 + LANE]]
        blocks += [main_ref[j * ATTN_BLOCK:(j + 1) * ATTN_BLOCK, c0:c0 + LANE] for j in range(nblk)]
        blocks += [next_ref[:, c0:c0 + LANE], ctx_ref[:, c0:c0 + LANE]]
        return [_split_halves(blk) for blk in blocks]

    ri = lax.broadcasted_iota(jnp.int32, (ATTN_BLOCK, ATTN_BLOCK), 0)
    ci = lax.broadcasted_iota(jnp.int32, (ATTN_BLOCK, ATTN_BLOCK), 1)
    lower_ok = ci >= ri
    upper_ok = ci <= ri
    first_ok = jnp.logical_and(lower_ok, ti > 0)
    last_ok = jnp.logical_and(upper_ok, ti < tps - 1)
    contract_last = (((1,), (1,)), ((), ()))

    mix_ref[:, :POOL_WIDTH] = yp_ref[...]
    for g in range(N_KV_HEADS):
        kblocks = row_blocks(km_ref, kp_ref, kn_ref, kc_ref, g * LANE)
        vblocks = row_blocks(vm_ref, vp_ref, vn_ref, vc_ref, g * LANE)
        for j in range(nblk):
            sel = [j, j + 1, j + 2, nblk + 2]
            keys = [jnp.concatenate([kblocks[s][par] for s in sel], axis=0) for par in range(2)]
            vals = [jnp.concatenate([vblocks[s][par] for s in sel], axis=0) for par in range(2)]
            rows = slice(j * ATTN_BLOCK, (j + 1) * ATTN_BLOCK)
            qg = jnp.concatenate([q_ref[rows, (2 * g) * LANE:(2 * g + 1) * LANE],
                                  q_ref[rows, (2 * g + 1) * LANE:(2 * g + 2) * LANE]], axis=0)
            m_prev = first_ok if j == 0 else lower_ok
            m_next = last_ok if j == nblk - 1 else upper_ok
            probs = []
            inv_l = []
            for par in range(2):
                s_all = lax.dot_general(qg, keys[par], contract_last, preferred_element_type=f32)
                p_rows = []
                l_rows = []
                for pr in range(2):
                    sink = sink_ref[4 * g + 2 * pr + par]
                    s = s_all[pr * ATTN_BLOCK:(pr + 1) * ATTN_BLOCK]
                    s = jnp.concatenate([
                        jnp.where(m_prev, s[:, :ATTN_BLOCK], NEG_INF),
                        s[:, ATTN_BLOCK:2 * ATTN_BLOCK],
                        jnp.where(m_next, s[:, 2 * ATTN_BLOCK:3 * ATTN_BLOCK], NEG_INF),
                        s[:, 3 * ATTN_BLOCK:]], axis=1)
                    m = jnp.maximum(jnp.max(s, axis=-1, keepdims=True), sink)
                    p = jnp.exp(s - m)
                    l = jnp.sum(p, axis=-1, keepdims=True) + jnp.exp(sink - m)
                    p_rows.append(p.astype(bf16))
                    l_rows.append(1.0 / l)
                probs.append(jnp.concatenate(p_rows, axis=0))
                inv_l.append(jnp.concatenate(l_rows, axis=0))
            o = (jnp.dot(probs[0], vals[0], preferred_element_type=f32)
                 + jnp.dot(probs[1], vals[1], preferred_element_type=f32))
            low = lax.broadcasted_iota(jnp.int32, o.shape, 1) < HEAD_DIM
            o = o * jnp.where(low, inv_l[0], inv_l[1])
            for pr in range(2):
                c0 = POOL_WIDTH + (2 * g + pr) * LANE
                mix_ref[rows, c0:c0 + LANE] = o[pr * ATTN_BLOCK:(pr + 1) * ATTN_BLOCK].astype(bf16)

    y = jnp.dot(mix_ref[...], wout_ref[...], preferred_element_type=f32)
    o_ref[...] = x_ref[...] + _mod_row(mods_ref, GT_M, b) * _rms(y, g_ref[...])


def _even_attn(sink, q, k, v, kc, vc, yp, x2d, mods, g_post, w_out):
    per = TQ // ATTN_BLOCK
    last = N_TOK // ATTN_BLOCK - 1
    tps = SEQ // TQ
    band = [
        pl.BlockSpec((TQ, KV_DUP), lambda i: (i, 0)),
        pl.BlockSpec((ATTN_BLOCK, KV_DUP), lambda i: (jnp.maximum(i * per - 1, 0), 0)),
        pl.BlockSpec((ATTN_BLOCK, KV_DUP), lambda i: (jnp.minimum((i + 1) * per, last), 0)),
    ]
    ctx_spec = pl.BlockSpec((CTX_LEN, KV_DUP), lambda i: (i // tps, 0))
    return pl.pallas_call(
        _even_attn_kernel,
        out_shape=jax.ShapeDtypeStruct((N_TOK, D_MODEL), f32),
        grid=(N_TOK // TQ,),
        in_specs=[pl.BlockSpec(memory_space=pltpu.SMEM),
                  pl.BlockSpec((TQ, ATTN_WIDTH), lambda i: (i, 0))]
                 + band + band + [
            ctx_spec, ctx_spec,
            pl.BlockSpec((TQ, POOL_WIDTH), lambda i: (i, 0)),
            pl.BlockSpec((TQ, D_MODEL), lambda i: (i, 0)),
            _mods_spec(0),
            _vec_spec(D_MODEL),
            _const_spec(w_out.shape),
        ],
        out_specs=pl.BlockSpec((TQ, D_MODEL), lambda i: (i, 0)),
        scratch_shapes=[pltpu.VMEM((TQ, POOL_WIDTH + ATTN_WIDTH), bf16)],
        compiler_params=_params(1),
        name="even_attn",
    )(sink, q, k, k, k, v, v, v, kc, vc, yp, x2d, mods, g_post, w_out)


def _conv_ffn_kernel(xm_ref, xp_ref, xn_ref, mods_ref, gpre_ref, gpost_ref, wup_ref, cw_ref, cb_ref,
                     wdn_ref, o_ref, slab_ref, h_ref, act_ref):
    tm = TM_FFN
    nr = tm // SUBLANE
    nslab = D_MODEL // LANE
    b, ti, tps = _seq_tile(tm)
    g = gpre_ref[...]
    shift = _mod_row(mods_ref, SH_F, b)
    scale = _mod_row(mods_ref, SC_F, b)

    def to_slab(val):
        for j in range(nslab):
            for a in range(SUBLANE):
                slab_ref[j, a * SLAB_PITCH:a * SLAB_PITCH + nr, :] = val[a * nr:(a + 1) * nr, j * LANE:(j + 1) * LANE]

    to_slab(_pre(xm_ref[...], g, shift, scale))
    for r in range(0, nr, 2):
        rows = [jnp.concatenate([slab_ref[j, pl.ds(r + k, SUBLANE, stride=SLAB_PITCH), :] for j in range(nslab)],
                                axis=1) for k in range(2)]
        h_ref[r * SUBLANE:(r + 2) * SUBLANE, :] = jnp.concatenate(rows, axis=0).astype(bf16)
    hp = jnp.where(ti > 0, _pre(xp_ref[...], g, shift, scale), 0.0)
    hn = jnp.where(ti < tps - 1, _pre(xn_ref[...], g, shift, scale), 0.0)
    h_ref[tm:, :] = jnp.concatenate([hp, hn], axis=0).astype(bf16)
    h = h_ref[...]

    sub = lax.broadcasted_iota(jnp.int32, (SUBLANE, FF_CHUNK), 0)

    def conv(hu, c0):
        w = cw_ref[:, c0:c0 + FF_CHUNK]
        first = jnp.where(sub == 0, pltpu.roll(hu[tm:tm + SUBLANE], 1, axis=0),
                          pltpu.roll(hu[tm - SUBLANE:tm], 1, axis=0))
        last = jnp.where(sub == SUBLANE - 1, pltpu.roll(hu[tm + SUBLANE:], SUBLANE - 1, axis=0),
                         pltpu.roll(hu[:SUBLANE], SUBLANE - 1, axis=0))
        before = jnp.concatenate([first, hu[:tm - SUBLANE]], axis=0)
        after = jnp.concatenate([hu[SUBLANE:tm], last], axis=0)
        return before * w[0:1] + hu[:tm] * w[1:2] + after * w[2:3] + cb_ref[:, c0:c0 + FF_CHUNK]

    for c in range(D_FF // FF_CHUNK):
        g0 = c * FF_CHUNK
        u0 = D_FF + c * FF_CHUNK
        gate = conv(jnp.dot(h, wup_ref[:, g0:g0 + FF_CHUNK], preferred_element_type=f32), g0)
        up = conv(jnp.dot(h, wup_ref[:, u0:u0 + FF_CHUNK], preferred_element_type=f32), u0)
        act_ref[:, g0:g0 + FF_CHUNK] = (jax.nn.silu(gate) * up).astype(bf16)

    y = jnp.dot(act_ref[...], wdn_ref[...], preferred_element_type=f32)
    res = _mod_row(mods_ref, GT_F, b) * _rms(y, gpost_ref[...])
    for r in range(nr):
        for j in range(nslab):
            slab_ref[j, pl.ds(r, SUBLANE, stride=SLAB_PITCH), :] = (
                res[r * SUBLANE:(r + 1) * SUBLANE, j * LANE:(j + 1) * LANE])
    for j in range(nslab):
        for a in range(SUBLANE):
            rows = slice(a * nr, (a + 1) * nr)
            lanes = slice(j * LANE, (j + 1) * LANE)
            o_ref[rows, lanes] = xm_ref[rows, lanes] + slab_ref[j, a * SLAB_PITCH:a * SLAB_PITCH + nr, :]


def _conv_ffn(x2d, mods, layer, g_pre, g_post, w_up, conv_w, conv_b, w_down):
    tm = TM_FFN
    return pl.pallas_call(
        _conv_ffn_kernel,
        out_shape=jax.ShapeDtypeStruct((N_TOK, D_MODEL), f32),
        grid=(N_TOK // tm,),
        in_specs=_halo_specs(tm, D_MODEL, SUBLANE) + [
            _mods_spec(layer),
            _vec_spec(D_MODEL),
            _vec_spec(D_MODEL),
            _const_spec(w_up.shape),
            pl.BlockSpec(conv_w.shape, lambda i: (0, 0)),
            pl.BlockSpec(conv_b.shape, lambda i: (0, 0)),
            _const_spec(w_down.shape),
        ],
        out_specs=pl.BlockSpec((tm, D_MODEL), lambda i: (i, 0)),
        scratch_shapes=[pltpu.VMEM((D_MODEL // LANE, SUBLANE * SLAB_PITCH, LANE), f32),
                        pltpu.VMEM((tm + 2 * SUBLANE, D_MODEL), bf16),
                        pltpu.VMEM((tm, D_FF), bf16)],
        compiler_params=_params(1),
        name="conv_ffn",
    )(x2d, x2d, x2d, mods, g_pre, g_post, w_up, conv_w, conv_b, w_down)


def _odd_mixer_kernel(x_ref, mods_ref, gpre_ref, gpost_ref, win_ref, lng_ref, lnb_ref, ws_ref, bs_ref,
                      wout_ref, o_ref, gated_ref):
    tm = TM_ODD
    nch = tm // CHUNK
    gw = D_MODEL // N_SGU_GROUPS
    b, _, _ = _seq_tile(tm)
    x = x_ref[...]
    h = _pre(x, gpre_ref[...], _mod_row(mods_ref, SH_M, b), _mod_row(mods_ref, SC_M, b)).astype(bf16)
    z = jax.nn.gelu(jnp.dot(h, win_ref[...], preferred_element_type=f32))
    u = z[:, :D_MODEL]
    v = z[:, D_MODEL:]
    mu = jnp.mean(v, axis=-1, keepdims=True)
    vc = v - mu
    var = jnp.mean(vc * vc, axis=-1, keepdims=True)
    vn = (vc * lax.rsqrt(var + EPS) * lng_ref[...] + lnb_ref[...]).astype(bf16)
    for g in range(N_SGU_GROUPS):
        lanes = slice(g * gw, (g + 1) * gw)
        vg = jnp.concatenate([vn[n * CHUNK:(n + 1) * CHUNK, lanes] for n in range(nch)], axis=1)
        s = jnp.dot(ws_ref[g], vg, preferred_element_type=f32)
        for n in range(nch):
            rows = slice(n * CHUNK, (n + 1) * CHUNK)
            gated_ref[rows, lanes] = (u[rows, lanes] * (s[:, n * gw:(n + 1) * gw] + bs_ref[g])).astype(bf16)
    y = jnp.dot(gated_ref[...], wout_ref[...], preferred_element_type=f32)
    o_ref[...] = x + _mod_row(mods_ref, GT_M, b) * _rms(y, gpost_ref[...])


def _odd_mixer(x2d, mods, g_pre, g_post, w_in, ln_g, ln_b, w_s, b_s, w_out):
    tm = TM_ODD
    return pl.pallas_call(
        _odd_mixer_kernel,
        out_shape=jax.ShapeDtypeStruct((N_TOK, D_MODEL), f32),
        grid=(N_TOK // tm,),
        in_specs=[
            pl.BlockSpec((tm, D_MODEL), lambda i: (i, 0)),
            _mods_spec(1),
            _vec_spec(D_MODEL), _vec_spec(D_MODEL),
            _const_spec(w_in.shape),
            _vec_spec(D_MODEL), _vec_spec(D_MODEL),
            _const_spec(w_s.shape),
            _const_spec(b_s.shape),
            _const_spec(w_out.shape),
        ],
        out_specs=pl.BlockSpec((tm, D_MODEL), lambda i: (i, 0)),
        scratch_shapes=[pltpu.VMEM((tm, D_MODEL), bf16)],
        compiler_params=_params(1),
        name="odd_mixer",
    )(x2d, mods, g_pre, g_post, w_in, ln_g, ln_b, w_s, b_s, w_out)


def _rope_tables():
    rows = SEQ // GRID_W
    row = jnp.repeat(jnp.arange(rows), GRID_W).astype(f32)
    col = jnp.tile(jnp.arange(GRID_W), rows).astype(f32)
    inv = ROPE_BASE ** (-jnp.arange(ROPE_FREQS, dtype=f32) / ROPE_FREQS)
    ar = row[:, None] * inv
    ac = col[:, None] * inv
    cos = jnp.concatenate([jnp.cos(ar), jnp.cos(ar), jnp.cos(ac), jnp.cos(ac)] * 2, axis=1)
    sin = jnp.concatenate([-jnp.sin(ar), jnp.sin(ar), -jnp.sin(ac), jnp.sin(ac)] * 2, axis=1)
    return cos, sin


def kernel(x, c, ctx, c_ctx, w_ada, b_ada, g_mix_pre, g_mix_post, g_ffn_pre, g_ffn_post, w_in_even, w_pool,
           pool_scale, attn_sink, w_out_even, w_in_odd, sgu_ln_g, sgu_ln_b, sgu_w, sgu_b, w_out_odd,
           w_ffn_up, ffn_conv_w, ffn_conv_b, w_ffn_down):
    assert x.shape == (BATCH, SEQ, D_MODEL) and ctx.shape == (BATCH, CTX_LEN, D_MODEL)
    x2d = x.reshape(N_TOK, D_MODEL)
    ctx2d = ctx.reshape(BATCH * CTX_LEN, D_MODEL)
    row = lambda a: a[None, :]

    cvec = jnp.concatenate([c, c_ctx[None, :], jnp.zeros((8 - BATCH - 1, D_MODEL), f32)], axis=0)
    mods = _adaln(cvec, w_ada, b_ada)

    w = w_in_even[0]
    dup = [w[:, Q_END + i * HEAD_DIM:Q_END + (i + 1) * HEAD_DIM] for i in range(2 * N_KV_HEADS)]
    w1 = jnp.concatenate([w[:, :Q_END]] + [d for d in dup for _ in range(2)], axis=1).astype(bf16)
    cos, sin = _rope_tables()
    kc, vc = _ctx_kv(ctx2d, mods, row(g_mix_pre[0]), w1[:, Q_END:])
    yp, q, k, v = _even_in(x2d, mods, row(g_mix_pre[0]), w1, cos, sin, w_pool[0].astype(bf16),
                           row(pool_scale[0]))
    x2d = _even_attn(attn_sink[0], q, k, v, kc, vc, yp, x2d, mods, row(g_mix_post[0]),
                     w_out_even[0].astype(bf16))
    x2d = _conv_ffn(x2d, mods, 0, row(g_ffn_pre[0]), row(g_ffn_post[0]), w_ffn_up[0].astype(bf16),
                    ffn_conv_w[0], row(ffn_conv_b[0]), w_ffn_down[0].astype(bf16))

    b_s = jnp.broadcast_to(sgu_b[0][:, :, None], (N_SGU_GROUPS, CHUNK, CHUNK))
    x2d = _odd_mixer(x2d, mods, row(g_mix_pre[1]), row(g_mix_post[1]), w_in_odd[0].astype(bf16),
                     row(sgu_ln_g[0]), row(sgu_ln_b[0]), sgu_w[0].astype(bf16), b_s,
                     w_out_odd[0].astype(bf16))
    x2d = _conv_ffn(x2d, mods, 1, row(g_ffn_pre[1]), row(g_ffn_post[1]), w_ffn_up[1].astype(bf16),
                    ffn_conv_w[1], row(ffn_conv_b[1]), w_ffn_down[1].astype(bf16))
    return x2d.reshape(BATCH, SEQ, D_MODEL)
```

```python
import math

import numpy as np
import jax
import jax.numpy as jnp
from jax import lax
from jax.experimental import pallas as pl
from jax.experimental.pallas import tpu as pltpu

D_MODEL = 1024
BATCH = 4
SEQ = 4096
GRID_W = 64
CTX_LEN = 256
EPS = 1e-6
NEG_INF = -1e30
N_MOD = 6
POOL_WINDOWS = (2, 4, 8, 16)
POOL_GROUP_DIM = 128
POOL_WIDTH = 512
HEAD_DIM = 64
N_Q_HEADS = 8
N_KV_HEADS = 2
GQA_GROUP = N_Q_HEADS // N_KV_HEADS
ATTN_WIDTH = 512
KV_WIDTH = 128
ATTN_BLOCK = 128
ROPE_BASE = 10000.0
ROPE_FREQS = 16
Q_END = POOL_WIDTH + ATTN_WIDTH
CHUNK = 128
N_SGU_GROUPS = 8
D_FF = 2816
LOG2_E = math.log2(math.e)
GELU_C = math.sqrt(2.0 / math.pi)

LANE = 128
SUBLANE = 8
HALO = 16
VMEM_LIMIT = 56 * 1024 * 1024

N_TOK = BATCH * SEQ
TM_IN = 1024
TQ = 1024
TM_FFN = 1024
TM_ODD = 1024
FF_CHUNK = 256
ODD_CHUNK = 512
SLAB_PITCH = TM_FFN // SUBLANE + SUBLANE

f32 = jnp.float32
bf16 = jnp.bfloat16

SH_M, SC_M, GT_M, SH_F, SC_F, GT_F = range(N_MOD)
CTX_ROW = BATCH


def _rms(x, g):
    return x * lax.rsqrt(jnp.mean(x * x, axis=-1, keepdims=True) + EPS) * g


def _pre(x, g, shift, scale):
    return _rms(x, g) * (1.0 + scale) + shift


def _mod_row(mods_ref, slot, row):
    return mods_ref[slot, pl.ds(row, 1), :]


def _layer_spec(shape, layer, single_buffer=False):
    zeros = (0,) * (len(shape) - 1)
    mode = dict(pipeline_mode=pl.Buffered(1)) if single_buffer else {}
    return pl.BlockSpec((None,) + tuple(shape[1:]), lambda *_: (layer,) + zeros, **mode)


def _mods_spec(layer):
    return pl.BlockSpec((None, N_MOD, 8, D_MODEL), lambda *_: (layer, 0, 0, 0))


def _params(n_axes=1):
    return pltpu.CompilerParams(dimension_semantics=("arbitrary",) * n_axes,
                                vmem_limit_bytes=VMEM_LIMIT)


def _stack_rows(a):
    return a.reshape(a.shape[0], 1, a.shape[1])


def _adaln_kernel(c_ref, w_ref, b_ref, o_ref):
    s = jax.nn.silu(c_ref[...]).astype(bf16)
    o_ref[...] = jnp.dot(s, w_ref[...].astype(bf16), preferred_element_type=f32) + b_ref[...]


def _adaln(cvec, w_ada, b_ada):
    depth = w_ada.shape[0]
    return pl.pallas_call(
        _adaln_kernel,
        out_shape=jax.ShapeDtypeStruct((depth, N_MOD, 8, D_MODEL), f32),
        grid=(depth, N_MOD),
        in_specs=[
            pl.BlockSpec((8, D_MODEL), lambda l, j: (0, 0)),
            pl.BlockSpec((None, D_MODEL, D_MODEL), lambda l, j: (l, 0, j)),
            pl.BlockSpec((None, None, 1, D_MODEL), lambda l, j: (l, j, 0, 0)),
        ],
        out_specs=pl.BlockSpec((None, None, 8, D_MODEL), lambda l, j: (l, j, 0, 0)),
        compiler_params=_params(2),
        name="adaln",
    )(cvec, w_ada, b_ada.reshape(depth, N_MOD, 1, D_MODEL))


def _store_kv(k, v, k_ref, v_ref):
    low = lax.broadcasted_iota(jnp.int32, v.shape, 1) < HEAD_DIM
    swapped = pltpu.roll(v, HEAD_DIM, axis=1)
    for head, vals in enumerate((v, swapped)):
        k_ref[head] = k[:, head * HEAD_DIM:(head + 1) * HEAD_DIM].astype(bf16)
        v_ref[head] = jnp.where(low, vals, 1.0).astype(bf16)


def _ctx_kv_kernel(x_ref, mods_ref, g_ref, w_ref, k_ref, v_ref):
    h = _pre(x_ref[...], g_ref[...], mods_ref[SH_M, CTX_ROW:CTX_ROW + 1, :],
             mods_ref[SC_M, CTX_ROW:CTX_ROW + 1, :]).astype(bf16)
    z = jnp.dot(h, w_ref[:, Q_END:], preferred_element_type=f32)
    _store_kv(z[:, :KV_WIDTH], z[:, KV_WIDTH:], k_ref, v_ref)


def _ctx_kv(ctx2d, mods, g, w_in):
    n = ctx2d.shape[0]
    return pl.pallas_call(
        _ctx_kv_kernel,
        out_shape=(jax.ShapeDtypeStruct((N_KV_HEADS, n, HEAD_DIM), bf16),
                   jax.ShapeDtypeStruct((N_KV_HEADS, n, LANE), bf16)),
        grid=(1,),
        in_specs=[
            pl.BlockSpec((n, D_MODEL), lambda i: (0, 0)),
            _mods_spec(0),
            _layer_spec(g.shape, 0),
            _layer_spec(w_in.shape, 0),
        ],
        out_specs=(pl.BlockSpec((N_KV_HEADS, n, HEAD_DIM), lambda i: (0, 0, 0)),
                   pl.BlockSpec((N_KV_HEADS, n, LANE), lambda i: (0, 0, 0))),
        compiler_params=_params(1),
        name="ctx_kv",
    )(ctx2d, mods, g, w_in)


def _halo_specs(tm, width, halo=HALO):
    per = tm // halo
    last = N_TOK // halo - 1
    return [
        pl.BlockSpec((tm, width), lambda i: (i, 0)),
        pl.BlockSpec((halo, width), lambda i: (jnp.maximum(i * per - 1, 0), 0)),
        pl.BlockSpec((halo, width), lambda i: (jnp.minimum((i + 1) * per, last), 0)),
    ]


def _seq_tile(tm):
    tps = SEQ // tm
    i = pl.program_id(0)
    return i // tps, i % tps, tps


def _normed_with_halo(xm_ref, xp_ref, xn_ref, g, shift, scale, tm):
    _, ti, tps = _seq_tile(tm)
    hp = _pre(xp_ref[...], g, shift, scale)
    hn = _pre(xn_ref[...], g, shift, scale)
    hp = jnp.where(ti > 0, hp, 0.0)
    hn = jnp.where(ti < tps - 1, hn, 0.0)
    hm = _pre(xm_ref[...], g, shift, scale)
    return jnp.concatenate([hp, hm, hn], axis=0).astype(bf16)


def _rope(z, cos, sin_signed, lane):
    fwd = pltpu.roll(z, LANE - ROPE_FREQS, axis=1)
    bwd = pltpu.roll(z, ROPE_FREQS, axis=1)
    partner = jnp.where((lane % (2 * ROPE_FREQS)) < ROPE_FREQS, fwd, bwd)
    return z * cos + partner * sin_signed


def _even_in_kernel(xm_ref, xp_ref, xn_ref, mods_ref, g_ref, w_ref, cos_ref, sin_ref,
                    wpool_ref, pscale_ref, yp_ref, q_ref, k_ref, v_ref):
    tm = TM_IN
    b, ti, _ = _seq_tile(tm)
    shift = _mod_row(mods_ref, SH_M, b)
    scale = _mod_row(mods_ref, SC_M, b)
    h = _normed_with_halo(xm_ref, xp_ref, xn_ref, g_ref[...], shift, scale, tm)
    z = jnp.dot(h, w_ref[...], preferred_element_type=f32)

    zm = z[HALO:HALO + tm]
    cos = cos_ref[...]
    sin = sin_ref[...]
    lane = lax.broadcasted_iota(jnp.int32, (tm, LANE), 1)
    for j in range(ATTN_WIDTH // LANE):
        c0 = POOL_WIDTH + j * LANE
        pair = _rope(zm[:, c0:c0 + LANE], cos, sin, lane) * (HEAD_DIM ** -0.5 * LOG2_E)
        q_ref[2 * j] = pair[:, :HEAD_DIM].astype(bf16)
        q_ref[2 * j + 1] = pair[:, HEAD_DIM:].astype(bf16)
    _store_kv(_rope(zm[:, Q_END:Q_END + KV_WIDTH], cos, sin, lane), zm[:, Q_END + KV_WIDTH:], k_ref, v_ref)

    rows = tm + 2 * HALO
    pos = lax.broadcasted_iota(jnp.int32, (tm, POOL_GROUP_DIM), 0) + ti * tm
    for gi, win in enumerate(POOL_WINDOWS):
        lanes = slice(gi * POOL_GROUP_DIM, (gi + 1) * POOL_GROUP_DIM)
        u = z[:, lanes]
        s = u + pltpu.roll(u, 1, axis=0)
        step = 1
        while 2 * step < win:
            s = pltpu.roll(s, step, axis=0) + pltpu.roll(s, rows - step, axis=0)
            step *= 2
        half = win // 2
        cnt = (jnp.minimum(pos + half, SEQ) - jnp.maximum(pos - half, 0)).astype(f32)
        pooled = (s[HALO:HALO + tm] / cnt - u[HALO:HALO + tm]).astype(bf16)
        y = jnp.dot(pooled, wpool_ref[gi], preferred_element_type=f32)
        yp_ref[:, lanes] = (y * pscale_ref[:, lanes]).astype(bf16)


def _even_in(x2d, mods, g, w_in, cos, sin, w_pool, pool_scale):
    tm = TM_IN
    tps = SEQ // tm
    heads = lambda n, w: pl.BlockSpec((n, tm, w), lambda i: (0, i, 0))
    return pl.pallas_call(
        _even_in_kernel,
        out_shape=(jax.ShapeDtypeStruct((N_TOK, POOL_WIDTH), bf16),
                   jax.ShapeDtypeStruct((N_Q_HEADS, N_TOK, HEAD_DIM), bf16),
                   jax.ShapeDtypeStruct((N_KV_HEADS, N_TOK, HEAD_DIM), bf16),
                   jax.ShapeDtypeStruct((N_KV_HEADS, N_TOK, LANE), bf16)),
        grid=(N_TOK // tm,),
        in_specs=_halo_specs(tm, D_MODEL) + [
            _mods_spec(0),
            _layer_spec(g.shape, 0),
            _layer_spec(w_in.shape, 0, single_buffer=True),
            pl.BlockSpec((tm, LANE), lambda i: (i % tps, 0)),
            pl.BlockSpec((tm, LANE), lambda i: (i % tps, 0)),
            _layer_spec(w_pool.shape, 0, single_buffer=True),
            _layer_spec(pool_scale.shape, 0),
        ],
        out_specs=(pl.BlockSpec((tm, POOL_WIDTH), lambda i: (i, 0)), heads(N_Q_HEADS, HEAD_DIM),
                   heads(N_KV_HEADS, HEAD_DIM), heads(N_KV_HEADS, LANE)),
        compiler_params=_params(1),
        name="even_in",
    )(x2d, x2d, x2d, mods, g, w_in, cos, sin, w_pool, pool_scale)


def _even_attn_kernel(sink_ref, q_ref, km_ref, kp_ref, kn_ref, vm_ref, vp_ref, vn_ref,
                      kc_ref, vc_ref, yp_ref, x_ref, mods_ref, g_ref, wout_ref, o_ref, mix_ref):
    b, ti, tps = _seq_tile(TQ)
    nblk = TQ // ATTN_BLOCK

    def row_blocks(main_ref, prev_ref, next_ref, ctx_ref, g):
        blocks = [prev_ref[g]]
        blocks += [main_ref[g, j * ATTN_BLOCK:(j + 1) * ATTN_BLOCK, :] for j in range(nblk)]
        return blocks + [next_ref[g], ctx_ref[g]]

    ri = lax.broadcasted_iota(jnp.int32, (ATTN_BLOCK, ATTN_BLOCK), 0)
    ci = lax.broadcasted_iota(jnp.int32, (ATTN_BLOCK, ATTN_BLOCK), 1)
    lower_ok = ci >= ri
    upper_ok = ci <= ri
    first_ok = jnp.logical_and(lower_ok, ti > 0)
    last_ok = jnp.logical_and(upper_ok, ti < tps - 1)
    contract_last = (((1,), (1,)), ((), ()))
    low = lax.broadcasted_iota(jnp.int32, (ATTN_BLOCK, LANE), 1) < HEAD_DIM

    mix_ref[:, :POOL_WIDTH] = yp_ref[...]
    for g in range(N_KV_HEADS):
        kblocks = row_blocks(km_ref, kp_ref, kn_ref, kc_ref, g)
        vblocks = row_blocks(vm_ref, vp_ref, vn_ref, vc_ref, g)
        for j in range(nblk):
            sel = [j, j + 1, j + 2, nblk + 2]
            rows = slice(j * ATTN_BLOCK, (j + 1) * ATTN_BLOCK)
            keys = jnp.concatenate([kblocks[s] for s in sel], axis=0)
            vals = jnp.concatenate([vblocks[s] for s in sel], axis=0)
            qg = jnp.concatenate([q_ref[GQA_GROUP * g + hh, rows, :] for hh in range(GQA_GROUP)], axis=0)
            s_all = lax.dot_general(qg, keys, contract_last, preferred_element_type=f32)
            m_prev = first_ok if j == 0 else lower_ok
            m_next = last_ok if j == nblk - 1 else upper_ok
            p_rows = []
            sink_terms = []
            for hh in range(GQA_GROUP):
                sink = sink_ref[GQA_GROUP * g + hh] * LOG2_E
                s = s_all[hh * ATTN_BLOCK:(hh + 1) * ATTN_BLOCK]
                s = jnp.concatenate([
                    jnp.where(m_prev, s[:, :ATTN_BLOCK], NEG_INF),
                    s[:, ATTN_BLOCK:2 * ATTN_BLOCK],
                    jnp.where(m_next, s[:, 2 * ATTN_BLOCK:3 * ATTN_BLOCK], NEG_INF),
                    s[:, 3 * ATTN_BLOCK:]], axis=1)
                m = jnp.maximum(jnp.max(s, axis=-1, keepdims=True), sink)
                p_rows.append(jnp.exp2(s - m).astype(bf16))
                sink_terms.append(jnp.exp2(sink - m))
            out = jnp.dot(jnp.concatenate(p_rows, axis=0), vals, preferred_element_type=f32)
            for pr in range(GQA_GROUP // 2):
                even = out[(2 * pr) * ATTN_BLOCK:(2 * pr + 1) * ATTN_BLOCK]
                odd = out[(2 * pr + 1) * ATTN_BLOCK:(2 * pr + 2) * ATTN_BLOCK]
                o = jnp.where(low, even / (pltpu.roll(even, HEAD_DIM, axis=1) + sink_terms[2 * pr]),
                              pltpu.roll(odd, HEAD_DIM, axis=1) / (odd + sink_terms[2 * pr + 1]))
                c0 = POOL_WIDTH + (2 * g + pr) * LANE
                mix_ref[rows, c0:c0 + LANE] = o.astype(bf16)

    y = jnp.dot(mix_ref[...], wout_ref[...], preferred_element_type=f32)
    o_ref[...] = x_ref[...] + _mod_row(mods_ref, GT_M, b) * _rms(y, g_ref[...])


def _even_attn(sink, q, k, v, kc, vc, yp, x2d, mods, g_post, w_out):
    per = TQ // ATTN_BLOCK
    last = N_TOK // ATTN_BLOCK - 1
    tps = SEQ // TQ
    def band(width):
        return [
            pl.BlockSpec((N_KV_HEADS, TQ, width), lambda i: (0, i, 0)),
            pl.BlockSpec((N_KV_HEADS, ATTN_BLOCK, width), lambda i: (0, jnp.maximum(i * per - 1, 0), 0)),
            pl.BlockSpec((N_KV_HEADS, ATTN_BLOCK, width), lambda i: (0, jnp.minimum((i + 1) * per, last), 0)),
        ]

    ctx_spec = lambda width: pl.BlockSpec((N_KV_HEADS, CTX_LEN, width), lambda i: (0, i // tps, 0))
    return pl.pallas_call(
        _even_attn_kernel,
        out_shape=jax.ShapeDtypeStruct((N_TOK, D_MODEL), f32),
        grid=(N_TOK // TQ,),
        in_specs=[pl.BlockSpec(memory_space=pltpu.SMEM),
                  pl.BlockSpec((N_Q_HEADS, TQ, HEAD_DIM), lambda i: (0, i, 0))]
                 + band(HEAD_DIM) + band(LANE) + [
            ctx_spec(HEAD_DIM), ctx_spec(LANE),
            pl.BlockSpec((TQ, POOL_WIDTH), lambda i: (i, 0)),
            pl.BlockSpec((TQ, D_MODEL), lambda i: (i, 0)),
            _mods_spec(0),
            _layer_spec(g_post.shape, 0),
            _layer_spec(w_out.shape, 0, single_buffer=True),
        ],
        out_specs=pl.BlockSpec((TQ, D_MODEL), lambda i: (i, 0)),
        scratch_shapes=[pltpu.VMEM((TQ, POOL_WIDTH + ATTN_WIDTH), bf16)],
        compiler_params=_params(1),
        name="even_attn",
    )(sink, q, k, k, k, v, v, v, kc, vc, yp, x2d, mods, g_post, w_out)


def _conv_ffn_kernel(xm_ref, xp_ref, xn_ref, mods_ref, gpre_ref, gpost_ref, wup_ref, cw_ref, cb_ref,
                     wdn_ref, o_ref, slab_ref, h_ref, act_ref):
    tm = TM_FFN
    nr = tm // SUBLANE
    nslab = D_MODEL // LANE
    b, ti, tps = _seq_tile(tm)
    sub = lax.broadcasted_iota(jnp.int32, (SUBLANE, FF_CHUNK), 0)

    def conv(hu, c0):
        w = cw_ref[:, c0:c0 + FF_CHUNK]
        first = jnp.where(sub == 0, pltpu.roll(hu[tm:tm + SUBLANE], 1, axis=0),
                          pltpu.roll(hu[tm - SUBLANE:tm], 1, axis=0))
        last = jnp.where(sub == SUBLANE - 1, pltpu.roll(hu[tm + SUBLANE:], SUBLANE - 1, axis=0),
                         pltpu.roll(hu[:SUBLANE], SUBLANE - 1, axis=0))
        before = jnp.concatenate([first, hu[:tm - SUBLANE]], axis=0)
        after = jnp.concatenate([hu[SUBLANE:tm], last], axis=0)
        return before * w[0:1] + hu[:tm] * w[1:2] + after * w[2:3] + cb_ref[:, c0:c0 + FF_CHUNK]

    g = gpre_ref[...]
    shift = _mod_row(mods_ref, SH_F, b)
    scale = _mod_row(mods_ref, SC_F, b)
    val = _pre(xm_ref[...], g, shift, scale)
    for j in range(nslab):
        for a in range(SUBLANE):
            slab_ref[j, a * SLAB_PITCH:a * SLAB_PITCH + nr, :] = val[a * nr:(a + 1) * nr, j * LANE:(j + 1) * LANE]
    for r in range(0, nr, 2):
        rows = [jnp.concatenate([slab_ref[j, pl.ds(r + k, SUBLANE, stride=SLAB_PITCH), :]
                                 for j in range(nslab)], axis=1) for k in range(2)]
        h_ref[r * SUBLANE:(r + 2) * SUBLANE, :] = jnp.concatenate(rows, axis=0).astype(bf16)
    hp = jnp.where(ti > 0, _pre(xp_ref[...], g, shift, scale), 0.0)
    hn = jnp.where(ti < tps - 1, _pre(xn_ref[...], g, shift, scale), 0.0)
    h_ref[tm:, :] = jnp.concatenate([hp, hn], axis=0).astype(bf16)
    h = h_ref[...]
    for c in range(D_FF // FF_CHUNK):
        g0 = c * FF_CHUNK
        u0 = D_FF + c * FF_CHUNK
        gate = conv(jnp.dot(h, wup_ref[:, g0:g0 + FF_CHUNK], preferred_element_type=f32), g0)
        up = conv(jnp.dot(h, wup_ref[:, u0:u0 + FF_CHUNK], preferred_element_type=f32), u0)
        act_ref[:, g0:g0 + FF_CHUNK] = (jax.nn.silu(gate) * up).astype(bf16)

    y = jnp.dot(act_ref[...], wdn_ref[...], preferred_element_type=f32)
    res = _mod_row(mods_ref, GT_F, b) * _rms(y, gpost_ref[...])
    for r in range(nr):
        for j in range(nslab):
            slab_ref[j, pl.ds(r, SUBLANE, stride=SLAB_PITCH), :] = (
                res[r * SUBLANE:(r + 1) * SUBLANE, j * LANE:(j + 1) * LANE])
    for j in range(nslab):
        for a in range(SUBLANE):
            rows = slice(a * nr, (a + 1) * nr)
            lanes = slice(j * LANE, (j + 1) * LANE)
            o_ref[rows, lanes] = xm_ref[rows, lanes] + slab_ref[j, a * SLAB_PITCH:a * SLAB_PITCH + nr, :]


def _conv_ffn(x2d, mods, layer, g_pre, g_post, w_up, conv_w, conv_b, w_down):
    tm = TM_FFN
    return pl.pallas_call(
        _conv_ffn_kernel,
        out_shape=jax.ShapeDtypeStruct((N_TOK, D_MODEL), f32),
        grid=(N_TOK // tm,),
        in_specs=_halo_specs(tm, D_MODEL, SUBLANE) + [
            _mods_spec(layer),
            _layer_spec(g_pre.shape, layer),
            _layer_spec(g_post.shape, layer),
            _layer_spec(w_up.shape, layer, single_buffer=True),
            _layer_spec(conv_w.shape, layer),
            _layer_spec(conv_b.shape, layer),
            _layer_spec(w_down.shape, layer, single_buffer=True),
        ],
        out_specs=pl.BlockSpec((tm, D_MODEL), lambda i: (i, 0)),
        scratch_shapes=[pltpu.VMEM((D_MODEL // LANE, SUBLANE * SLAB_PITCH, LANE), f32),
                        pltpu.VMEM((tm + 2 * SUBLANE, D_MODEL), bf16),
                        pltpu.VMEM((tm, D_FF), bf16)],
        compiler_params=_params(1),
        name="conv_ffn",
    )(x2d, x2d, x2d, mods, g_pre, g_post, w_up, conv_w, conv_b, w_down)


def _gelu_tanh(x):
    return x * jax.nn.sigmoid(x * (2.0 * GELU_C + (2.0 * GELU_C * 0.044715) * (x * x)))


def _odd_mixer_kernel(x_ref, mods_ref, gpre_ref, gpost_ref, win_ref, lng_ref, lnb_ref, ws_ref, bs_ref,
                      wout_ref, o_ref, gated_ref):
    tm = TM_ODD
    nch = tm // CHUNK
    gw = D_MODEL // N_SGU_GROUPS
    b, _, _ = _seq_tile(tm)
    x = x_ref[...]
    h = _pre(x, gpre_ref[...], _mod_row(mods_ref, SH_M, b), _mod_row(mods_ref, SC_M, b)).astype(bf16)
    z = jnp.concatenate(
        [_gelu_tanh(jnp.dot(h, win_ref[:, c0:c0 + ODD_CHUNK], preferred_element_type=f32))
         for c0 in range(0, 2 * D_MODEL, ODD_CHUNK)], axis=1)
    u = z[:, :D_MODEL]
    v = z[:, D_MODEL:]
    mu = jnp.mean(v, axis=-1, keepdims=True)
    vc = v - mu
    var = jnp.mean(vc * vc, axis=-1, keepdims=True)
    vn = (vc * lax.rsqrt(var + EPS) * lng_ref[...] + lnb_ref[...]).astype(bf16)
    for g in range(N_SGU_GROUPS):
        lanes = slice(g * gw, (g + 1) * gw)
        vg = jnp.concatenate([vn[n * CHUNK:(n + 1) * CHUNK, lanes] for n in range(nch)], axis=1)
        s = jnp.dot(ws_ref[g], vg, preferred_element_type=f32)
        for n in range(nch):
            rows = slice(n * CHUNK, (n + 1) * CHUNK)
            gated_ref[rows, lanes] = (u[rows, lanes] * (s[:, n * gw:(n + 1) * gw] + bs_ref[g])).astype(bf16)
    y = jnp.dot(gated_ref[...], wout_ref[...], preferred_element_type=f32)
    o_ref[...] = x + _mod_row(mods_ref, GT_M, b) * _rms(y, gpost_ref[...])


def _odd_mixer(x2d, mods, g_pre, g_post, w_in, ln_g, ln_b, w_s, b_s, w_out):
    tm = TM_ODD
    return pl.pallas_call(
        _odd_mixer_kernel,
        out_shape=jax.ShapeDtypeStruct((N_TOK, D_MODEL), f32),
        grid=(N_TOK // tm,),
        in_specs=[
            pl.BlockSpec((tm, D_MODEL), lambda i: (i, 0)),
            _mods_spec(1),
            _layer_spec(g_pre.shape, 1),
            _layer_spec(g_post.shape, 1),
            _layer_spec(w_in.shape, 0, single_buffer=True),
            _layer_spec(ln_g.shape, 0),
            _layer_spec(ln_b.shape, 0),
            _layer_spec(w_s.shape, 0, single_buffer=True),
            pl.BlockSpec(b_s.shape, lambda i: (0, 0, 0), pipeline_mode=pl.Buffered(1)),
            _layer_spec(w_out.shape, 0, single_buffer=True),
        ],
        out_specs=pl.BlockSpec((tm, D_MODEL), lambda i: (i, 0)),
        scratch_shapes=[pltpu.VMEM((tm, D_MODEL), bf16)],
        compiler_params=_params(1),
        name="odd_mixer",
    )(x2d, mods, g_pre, g_post, w_in, ln_g, ln_b, w_s, b_s, w_out)


def _rope_tables():
    pos = np.arange(SEQ)
    inv = (np.float32(ROPE_BASE) ** (-np.arange(ROPE_FREQS, dtype=np.float32) / ROPE_FREQS)).astype(np.float32)
    ar = (pos // GRID_W).astype(np.float32)[:, None] * inv
    ac = (pos % GRID_W).astype(np.float32)[:, None] * inv
    cos = np.concatenate([np.cos(ar), np.cos(ar), np.cos(ac), np.cos(ac)] * 2, axis=1)
    sin = np.concatenate([-np.sin(ar), np.sin(ar), -np.sin(ac), np.sin(ac)] * 2, axis=1)
    return cos.astype(np.float32), sin.astype(np.float32)


_ROPE_COS, _ROPE_SIN = _rope_tables()


def kernel(x, c, ctx, c_ctx, w_ada, b_ada, g_mix_pre, g_mix_post, g_ffn_pre, g_ffn_post, w_in_even, w_pool,
           pool_scale, attn_sink, w_out_even, w_in_odd, sgu_ln_g, sgu_ln_b, sgu_w, sgu_b, w_out_odd,
           w_ffn_up, ffn_conv_w, ffn_conv_b, w_ffn_down):
    assert x.shape == (BATCH, SEQ, D_MODEL) and ctx.shape == (BATCH, CTX_LEN, D_MODEL)
    x2d = x.reshape(N_TOK, D_MODEL)
    ctx2d = ctx.reshape(BATCH * CTX_LEN, D_MODEL)

    cvec = jnp.concatenate([c, c_ctx[None, :], jnp.zeros((8 - BATCH - 1, D_MODEL), f32)], axis=0)
    mods = _adaln(cvec, w_ada, b_ada)

    g_mix_pre, g_mix_post, g_ffn_pre, g_ffn_post = map(_stack_rows, (g_mix_pre, g_mix_post, g_ffn_pre, g_ffn_post))
    w_up = w_ffn_up.astype(bf16)
    w_down = w_ffn_down.astype(bf16)
    conv_b = _stack_rows(ffn_conv_b)

    w_in = w_in_even.astype(bf16)
    kc, vc = _ctx_kv(ctx2d, mods, g_mix_pre, w_in)
    yp, q, k, v = _even_in(x2d, mods, g_mix_pre, w_in, jnp.asarray(_ROPE_COS), jnp.asarray(_ROPE_SIN),
                           w_pool.astype(bf16), _stack_rows(pool_scale))
    x2d = _even_attn(attn_sink[0], q, k, v, kc, vc, yp, x2d, mods, g_mix_post, w_out_even.astype(bf16))
    x2d = _conv_ffn(x2d, mods, 0, g_ffn_pre, g_ffn_post, w_up, ffn_conv_w, conv_b, w_down)

    b_s = jnp.broadcast_to(sgu_b[0][:, :, None], (N_SGU_GROUPS, CHUNK, CHUNK))
    x2d = _odd_mixer(x2d, mods, g_mix_pre, g_mix_post, w_in_odd.astype(bf16), _stack_rows(sgu_ln_g),
                     _stack_rows(sgu_ln_b), sgu_w.astype(bf16), b_s, w_out_odd.astype(bf16))
    x2d = _conv_ffn(x2d, mods, 1, g_ffn_pre, g_ffn_post, w_up, ffn_conv_w, conv_b, w_down)
    return x2d.reshape(BATCH, SEQ, D_MODEL)
```

```python
import math

import numpy as np
import jax
import jax.numpy as jnp
from jax import lax
from jax.experimental import pallas as pl
from jax.experimental.pallas import tpu as pltpu

D_MODEL = 1024
BATCH = 4
SEQ = 4096
GRID_W = 64
CTX_LEN = 256
EPS = 1e-6
NEG_INF = -1e30
N_MOD = 6
POOL_WINDOWS = (2, 4, 8, 16)
POOL_GROUP_DIM = 128
POOL_WIDTH = 512
HEAD_DIM = 64
N_Q_HEADS = 8
N_KV_HEADS = 2
GQA_GROUP = N_Q_HEADS // N_KV_HEADS
ATTN_WIDTH = 512
KV_WIDTH = 128
ATTN_BLOCK = 128
ROPE_BASE = 10000.0
ROPE_FREQS = 16
Q_END = POOL_WIDTH + ATTN_WIDTH
CHUNK = 128
N_SGU_GROUPS = 8
D_FF = 2816
LOG2_E = math.log2(math.e)
GELU_C = math.sqrt(2.0 / math.pi)

LANE = 128
SUBLANE = 8
HALO = 16
VMEM_LIMIT = 56 * 1024 * 1024

N_TOK = BATCH * SEQ
TM_IN = 1024
TQ = 1024
TM_FFN = 1024
TM_ODD = 1024
FF_CHUNK = 256
ODD_CHUNK = 512
SLAB_PITCH = TM_FFN // SUBLANE + SUBLANE

f32 = jnp.float32
bf16 = jnp.bfloat16

SH_M, SC_M, GT_M, SH_F, SC_F, GT_F = range(N_MOD)
CTX_ROW = BATCH


def _rms(x, g):
    return x * lax.rsqrt(jnp.mean(x * x, axis=-1, keepdims=True) + EPS) * g


def _pre(x, g, shift, scale):
    return _rms(x, g * (1.0 + scale)) + shift


def _mod_row(mods_ref, slot, row):
    return mods_ref[slot, pl.ds(row, 1), :]


def _layer_spec(shape, layer, single_buffer=False):
    zeros = (0,) * (len(shape) - 1)
    mode = dict(pipeline_mode=pl.Buffered(1)) if single_buffer else {}
    return pl.BlockSpec((None,) + tuple(shape[1:]), lambda *_: (layer,) + zeros, **mode)


def _mods_spec(layer):
    return pl.BlockSpec((None, N_MOD, 8, D_MODEL), lambda *_: (layer, 0, 0, 0))


def _params(n_axes=1):
    return pltpu.CompilerParams(dimension_semantics=("arbitrary",) * n_axes,
                                vmem_limit_bytes=VMEM_LIMIT)


def _stack_rows(a):
    return a.reshape(a.shape[0], 1, a.shape[1])


def _adaln_kernel(c_ref, w_ref, b_ref, o_ref):
    s = jax.nn.silu(c_ref[...]).astype(bf16)
    o_ref[...] = jnp.dot(s, w_ref[...].astype(bf16), preferred_element_type=f32) + b_ref[...]


def _adaln(cvec, w_ada, b_ada):
    depth = w_ada.shape[0]
    return pl.pallas_call(
        _adaln_kernel,
        out_shape=jax.ShapeDtypeStruct((depth, N_MOD, 8, D_MODEL), f32),
        grid=(depth, N_MOD),
        in_specs=[
            pl.BlockSpec((8, D_MODEL), lambda l, j: (0, 0)),
            pl.BlockSpec((None, D_MODEL, D_MODEL), lambda l, j: (l, 0, j)),
            pl.BlockSpec((None, None, 1, D_MODEL), lambda l, j: (l, j, 0, 0)),
        ],
        out_specs=pl.BlockSpec((None, None, 8, D_MODEL), lambda l, j: (l, j, 0, 0)),
        compiler_params=_params(2),
        name="adaln",
    )(cvec, w_ada, b_ada.reshape(depth, N_MOD, 1, D_MODEL))


def _store_kv(k, v, k_ref, v_ref):
    low = lax.broadcasted_iota(jnp.int32, v.shape, 1) < HEAD_DIM
    swapped = pltpu.roll(v, HEAD_DIM, axis=1)
    kt = k.T.astype(bf16)
    for head, vals in enumerate((v, swapped)):
        k_ref[head] = kt[head * HEAD_DIM:(head + 1) * HEAD_DIM]
        v_ref[head] = jnp.where(low, vals, 1.0).astype(bf16)


def _ctx_kv_kernel(x_ref, mods_ref, g_ref, w_ref, k_ref, v_ref):
    h = _pre(x_ref[...], g_ref[...], mods_ref[SH_M, CTX_ROW:CTX_ROW + 1, :],
             mods_ref[SC_M, CTX_ROW:CTX_ROW + 1, :]).astype(bf16)
    z = jnp.dot(h, w_ref[:, Q_END:], preferred_element_type=f32)
    _store_kv(z[:, :KV_WIDTH], z[:, KV_WIDTH:], k_ref, v_ref)


def _ctx_kv(ctx2d, mods, g, w_in):
    n = ctx2d.shape[0]
    return pl.pallas_call(
        _ctx_kv_kernel,
        out_shape=(jax.ShapeDtypeStruct((N_KV_HEADS, HEAD_DIM, n), bf16),
                   jax.ShapeDtypeStruct((N_KV_HEADS, n, LANE), bf16)),
        grid=(1,),
        in_specs=[
            pl.BlockSpec((n, D_MODEL), lambda i: (0, 0)),
            _mods_spec(0),
            _layer_spec(g.shape, 0),
            _layer_spec(w_in.shape, 0),
        ],
        out_specs=(pl.BlockSpec((N_KV_HEADS, HEAD_DIM, n), lambda i: (0, 0, 0)),
                   pl.BlockSpec((N_KV_HEADS, n, LANE), lambda i: (0, 0, 0))),
        compiler_params=_params(1),
        name="ctx_kv",
    )(ctx2d, mods, g, w_in)


def _halo_specs(tm, width, halo=HALO):
    per = tm // halo
    last = N_TOK // halo - 1
    return [
        pl.BlockSpec((tm, width), lambda i: (i, 0)),
        pl.BlockSpec((halo, width), lambda i: (jnp.maximum(i * per - 1, 0), 0)),
        pl.BlockSpec((halo, width), lambda i: (jnp.minimum((i + 1) * per, last), 0)),
    ]


def _seq_tile(tm):
    tps = SEQ // tm
    i = pl.program_id(0)
    return i // tps, i % tps, tps


def _normed_with_halo(xm_ref, xp_ref, xn_ref, g, shift, scale, tm):
    _, ti, tps = _seq_tile(tm)
    hp = _pre(xp_ref[...], g, shift, scale)
    hn = _pre(xn_ref[...], g, shift, scale)
    hp = jnp.where(ti > 0, hp, 0.0)
    hn = jnp.where(ti < tps - 1, hn, 0.0)
    hm = _pre(xm_ref[...], g, shift, scale)
    return jnp.concatenate([hp, hm, hn], axis=0).astype(bf16)


def _rope(z, cos, sin_signed, lane):
    fwd = pltpu.roll(z, LANE - ROPE_FREQS, axis=1)
    bwd = pltpu.roll(z, ROPE_FREQS, axis=1)
    partner = jnp.where((lane % (2 * ROPE_FREQS)) < ROPE_FREQS, fwd, bwd)
    return z * cos + partner * sin_signed


def _even_in_kernel(xm_ref, xp_ref, xn_ref, mods_ref, g_ref, w_ref, rope_ref,
                    wpool_ref, pscale_ref, yp_ref, q_ref, k_ref, v_ref):
    tm = TM_IN
    b, ti, _ = _seq_tile(tm)
    shift = _mod_row(mods_ref, SH_M, b)
    scale = _mod_row(mods_ref, SC_M, b)
    h = _normed_with_halo(xm_ref, xp_ref, xn_ref, g_ref[...], shift, scale, tm)
    z = jnp.dot(h, w_ref[...], preferred_element_type=f32)

    zm = z[HALO:HALO + tm]
    lane = lax.broadcasted_iota(jnp.int32, (tm, LANE), 1)
    for j in range(ATTN_WIDTH // LANE):
        c0 = POOL_WIDTH + j * LANE
        pair = _rope(zm[:, c0:c0 + LANE], rope_ref[2], rope_ref[3], lane)
        q_ref[2 * j] = pair[:, :HEAD_DIM].astype(bf16)
        q_ref[2 * j + 1] = pair[:, HEAD_DIM:].astype(bf16)
    _store_kv(_rope(zm[:, Q_END:Q_END + KV_WIDTH], rope_ref[0], rope_ref[1], lane),
              zm[:, Q_END + KV_WIDTH:], k_ref, v_ref)

    rows = tm + 2 * HALO
    edge = lax.broadcasted_iota(jnp.int32, (SUBLANE, POOL_GROUP_DIM), 0) + ti * tm
    for gi, win in enumerate(POOL_WINDOWS):
        lanes = slice(gi * POOL_GROUP_DIM, (gi + 1) * POOL_GROUP_DIM)
        u = z[:, lanes]
        s = u + pltpu.roll(u, 1, axis=0)
        step = 1
        while 2 * step < win:
            s = pltpu.roll(s, step, axis=0) + pltpu.roll(s, rows - step, axis=0)
            step *= 2
        half = win // 2
        count = lambda pos: (jnp.minimum(pos + half, SEQ) - jnp.maximum(pos - half, 0)).astype(f32)
        sm = s[HALO:HALO + tm]
        mean = jnp.concatenate([sm[:SUBLANE] / count(edge),
                                sm[SUBLANE:tm - SUBLANE] * (1.0 / win),
                                sm[tm - SUBLANE:] / count(edge + (tm - SUBLANE))], axis=0)
        pooled = (mean - u[HALO:HALO + tm]).astype(bf16)
        y = jnp.dot(pooled, wpool_ref[gi], preferred_element_type=f32)
        yp_ref[:, lanes] = (y * pscale_ref[:, lanes]).astype(bf16)


def _even_in(x2d, mods, g, w_in, rope, w_pool, pool_scale):
    tm = TM_IN
    tps = SEQ // tm
    heads = lambda n, w: pl.BlockSpec((n, tm, w), lambda i: (0, i, 0))
    return pl.pallas_call(
        _even_in_kernel,
        out_shape=(jax.ShapeDtypeStruct((N_TOK, POOL_WIDTH), bf16),
                   jax.ShapeDtypeStruct((N_Q_HEADS, N_TOK, HEAD_DIM), bf16),
                   jax.ShapeDtypeStruct((N_KV_HEADS, HEAD_DIM, N_TOK), bf16),
                   jax.ShapeDtypeStruct((N_KV_HEADS, N_TOK, LANE), bf16)),
        grid=(N_TOK // tm,),
        in_specs=_halo_specs(tm, D_MODEL) + [
            _mods_spec(0),
            _layer_spec(g.shape, 0),
            _layer_spec(w_in.shape, 0, single_buffer=True),
            pl.BlockSpec((rope.shape[0], tm, LANE), lambda i: (0, i % tps, 0)),
            _layer_spec(w_pool.shape, 0, single_buffer=True),
            _layer_spec(pool_scale.shape, 0),
        ],
        out_specs=(pl.BlockSpec((tm, POOL_WIDTH), lambda i: (i, 0)), heads(N_Q_HEADS, HEAD_DIM),
                   pl.BlockSpec((N_KV_HEADS, HEAD_DIM, tm), lambda i: (0, 0, i)), heads(N_KV_HEADS, LANE)),
        compiler_params=_params(1),
        name="even_in",
    )(x2d, x2d, x2d, mods, g, w_in, rope, w_pool, pool_scale)


def _even_attn_kernel(sink_ref, q_ref, km_ref, kp_ref, kn_ref, vm_ref, vp_ref, vn_ref,
                      kc_ref, vc_ref, yp_ref, x_ref, mods_ref, g_ref, wout_ref, o_ref, mix_ref):
    b, ti, tps = _seq_tile(TQ)
    nblk = TQ // ATTN_BLOCK

    def row_blocks(main_ref, prev_ref, next_ref, ctx_ref, g):
        blocks = [prev_ref[g]]
        blocks += [main_ref[g, j * ATTN_BLOCK:(j + 1) * ATTN_BLOCK, :] for j in range(nblk)]
        return blocks + [next_ref[g], ctx_ref[g]]

    ri = lax.broadcasted_iota(jnp.int32, (ATTN_BLOCK, ATTN_BLOCK), 0)
    ci = lax.broadcasted_iota(jnp.int32, (ATTN_BLOCK, ATTN_BLOCK), 1)
    lower_ok = ci >= ri
    upper_ok = ci <= ri
    first_ok = jnp.logical_and(lower_ok, ti > 0)
    last_ok = jnp.logical_and(upper_ok, ti < tps - 1)
    low = lax.broadcasted_iota(jnp.int32, (ATTN_BLOCK, LANE), 1) < HEAD_DIM

    mix_ref[:, :POOL_WIDTH] = yp_ref[...]
    for g in range(N_KV_HEADS):
        kblocks = ([kp_ref[g]] + [km_ref[g, :, j * ATTN_BLOCK:(j + 1) * ATTN_BLOCK] for j in range(nblk)]
                   + [kn_ref[g], kc_ref[g]])
        vblocks = row_blocks(vm_ref, vp_ref, vn_ref, vc_ref, g)
        for j in range(nblk):
            sel = [j, j + 1, j + 2, nblk + 2]
            rows = slice(j * ATTN_BLOCK, (j + 1) * ATTN_BLOCK)
            keys = jnp.concatenate([kblocks[s] for s in sel], axis=1)
            vals = jnp.concatenate([vblocks[s] for s in sel], axis=0)
            qg = jnp.concatenate([q_ref[GQA_GROUP * g + hh, rows, :] for hh in range(GQA_GROUP)], axis=0)
            s_all = jnp.dot(qg, keys, preferred_element_type=f32)
            m_prev = first_ok if j == 0 else lower_ok
            m_next = last_ok if j == nblk - 1 else upper_ok
            p_rows = []
            sink_terms = []
            for hh in range(GQA_GROUP):
                sink = sink_ref[GQA_GROUP * g + hh] * LOG2_E
                s = s_all[hh * ATTN_BLOCK:(hh + 1) * ATTN_BLOCK]
                s = jnp.concatenate([
                    jnp.where(m_prev, s[:, :ATTN_BLOCK], NEG_INF),
                    s[:, ATTN_BLOCK:2 * ATTN_BLOCK],
                    jnp.where(m_next, s[:, 2 * ATTN_BLOCK:3 * ATTN_BLOCK], NEG_INF),
                    s[:, 3 * ATTN_BLOCK:]], axis=1)
                m = jnp.maximum(jnp.max(s, axis=-1, keepdims=True), sink)
                p_rows.append(jnp.exp2(s - m).astype(bf16))
                sink_terms.append(jnp.exp2(sink - m))
            out = jnp.dot(jnp.concatenate(p_rows, axis=0), vals, preferred_element_type=f32)
            for pr in range(GQA_GROUP // 2):
                even = out[(2 * pr) * ATTN_BLOCK:(2 * pr + 1) * ATTN_BLOCK]
                odd = out[(2 * pr + 1) * ATTN_BLOCK:(2 * pr + 2) * ATTN_BLOCK]
                o = jnp.where(low, even / (pltpu.roll(even, HEAD_DIM, axis=1) + sink_terms[2 * pr]),
                              pltpu.roll(odd, HEAD_DIM, axis=1) / (odd + sink_terms[2 * pr + 1]))
                c0 = POOL_WIDTH + (2 * g + pr) * LANE
                mix_ref[rows, c0:c0 + LANE] = o.astype(bf16)

    y = jnp.dot(mix_ref[...], wout_ref[...], preferred_element_type=f32)
    o_ref[...] = x_ref[...] + _rms(y, _mod_row(mods_ref, GT_M, b) * g_ref[...])


def _even_attn(sink, q, k, v, kc, vc, yp, x2d, mods, g_post, w_out):
    per = TQ // ATTN_BLOCK
    last = N_TOK // ATTN_BLOCK - 1
    tps = SEQ // TQ
    prev_blk = lambda i: jnp.maximum(i * per - 1, 0)
    next_blk = lambda i: jnp.minimum((i + 1) * per, last)
    k_band = [
        pl.BlockSpec((N_KV_HEADS, HEAD_DIM, TQ), lambda i: (0, 0, i)),
        pl.BlockSpec((N_KV_HEADS, HEAD_DIM, ATTN_BLOCK), lambda i: (0, 0, prev_blk(i))),
        pl.BlockSpec((N_KV_HEADS, HEAD_DIM, ATTN_BLOCK), lambda i: (0, 0, next_blk(i))),
    ]
    v_band = [
        pl.BlockSpec((N_KV_HEADS, TQ, LANE), lambda i: (0, i, 0)),
        pl.BlockSpec((N_KV_HEADS, ATTN_BLOCK, LANE), lambda i: (0, prev_blk(i), 0)),
        pl.BlockSpec((N_KV_HEADS, ATTN_BLOCK, LANE), lambda i: (0, next_blk(i), 0)),
    ]
    return pl.pallas_call(
        _even_attn_kernel,
        out_shape=jax.ShapeDtypeStruct((N_TOK, D_MODEL), f32),
        grid=(N_TOK // TQ,),
        in_specs=[pl.BlockSpec(memory_space=pltpu.SMEM),
                  pl.BlockSpec((N_Q_HEADS, TQ, HEAD_DIM), lambda i: (0, i, 0))]
                 + k_band + v_band + [
            pl.BlockSpec((N_KV_HEADS, HEAD_DIM, CTX_LEN), lambda i: (0, 0, i // tps)),
            pl.BlockSpec((N_KV_HEADS, CTX_LEN, LANE), lambda i: (0, i // tps, 0)),
            pl.BlockSpec((TQ, POOL_WIDTH), lambda i: (i, 0)),
            pl.BlockSpec((TQ, D_MODEL), lambda i: (i, 0)),
            _mods_spec(0),
            _layer_spec(g_post.shape, 0),
            _layer_spec(w_out.shape, 0, single_buffer=True),
        ],
        out_specs=pl.BlockSpec((TQ, D_MODEL), lambda i: (i, 0)),
        scratch_shapes=[pltpu.VMEM((TQ, POOL_WIDTH + ATTN_WIDTH), bf16)],
        compiler_params=_params(1),
        name="even_attn",
    )(sink, q, k, k, k, v, v, v, kc, vc, yp, x2d, mods, g_post, w_out)


def _conv_ffn_kernel(xm_ref, xp_ref, xn_ref, mods_ref, gpre_ref, gpost_ref, wup_ref, cw_ref, cb_ref,
                     wdn_ref, o_ref, slab_ref, h_ref, act_ref):
    tm = TM_FFN
    nr = tm // SUBLANE
    nslab = D_MODEL // LANE
    b, ti, tps = _seq_tile(tm)
    sub = lax.broadcasted_iota(jnp.int32, (SUBLANE, FF_CHUNK), 0)

    def conv(hu, c0):
        w = cw_ref[:, c0:c0 + FF_CHUNK]
        first = jnp.where(sub == 0, pltpu.roll(hu[tm:tm + SUBLANE], 1, axis=0),
                          pltpu.roll(hu[tm - SUBLANE:tm], 1, axis=0))
        last = jnp.where(sub == SUBLANE - 1, pltpu.roll(hu[tm + SUBLANE:], SUBLANE - 1, axis=0),
                         pltpu.roll(hu[:SUBLANE], SUBLANE - 1, axis=0))
        before = jnp.concatenate([first, hu[:tm - SUBLANE]], axis=0)
        after = jnp.concatenate([hu[SUBLANE:tm], last], axis=0)
        return before * w[0:1] + hu[:tm] * w[1:2] + after * w[2:3] + cb_ref[:, c0:c0 + FF_CHUNK]

    g = gpre_ref[...]
    shift = _mod_row(mods_ref, SH_F, b)
    scale = _mod_row(mods_ref, SC_F, b)
    val = _pre(xm_ref[...], g, shift, scale)
    for j in range(nslab):
        for a in range(SUBLANE):
            slab_ref[j, a * SLAB_PITCH:a * SLAB_PITCH + nr, :] = val[a * nr:(a + 1) * nr, j * LANE:(j + 1) * LANE]
    for r in range(0, nr, 2):
        rows = [jnp.concatenate([slab_ref[j, pl.ds(r + k, SUBLANE, stride=SLAB_PITCH), :]
                                 for j in range(nslab)], axis=1) for k in range(2)]
        h_ref[r * SUBLANE:(r + 2) * SUBLANE, :] = jnp.concatenate(rows, axis=0).astype(bf16)
    hp = jnp.where(ti > 0, _pre(xp_ref[...], g, shift, scale), 0.0)
    hn = jnp.where(ti < tps - 1, _pre(xn_ref[...], g, shift, scale), 0.0)
    h_ref[tm:, :] = jnp.concatenate([hp, hn], axis=0).astype(bf16)
    h = h_ref[...]
    for c in range(D_FF // FF_CHUNK):
        g0 = c * FF_CHUNK
        u0 = D_FF + c * FF_CHUNK
        gate = conv(jnp.dot(h, wup_ref[:, g0:g0 + FF_CHUNK], preferred_element_type=f32), g0)
        up = conv(jnp.dot(h, wup_ref[:, u0:u0 + FF_CHUNK], preferred_element_type=f32), u0)
        act_ref[:, g0:g0 + FF_CHUNK] = (jax.nn.silu(gate) * up).astype(bf16)

    y = jnp.dot(act_ref[...], wdn_ref[...], preferred_element_type=f32)
    res = _rms(y, _mod_row(mods_ref, GT_F, b) * gpost_ref[...])
    for r in range(nr):
        for j in range(nslab):
            slab_ref[j, pl.ds(r, SUBLANE, stride=SLAB_PITCH), :] = (
                res[r * SUBLANE:(r + 1) * SUBLANE, j * LANE:(j + 1) * LANE])
    for j in range(nslab):
        for a in range(SUBLANE):
            rows = slice(a * nr, (a + 1) * nr)
            lanes = slice(j * LANE, (j + 1) * LANE)
            o_ref[rows, lanes] = xm_ref[rows, lanes] + slab_ref[j, a * SLAB_PITCH:a * SLAB_PITCH + nr, :]


def _conv_ffn(x2d, mods, layer, g_pre, g_post, w_up, conv_w, conv_b, w_down):
    tm = TM_FFN
    return pl.pallas_call(
        _conv_ffn_kernel,
        out_shape=jax.ShapeDtypeStruct((N_TOK, D_MODEL), f32),
        grid=(N_TOK // tm,),
        in_specs=_halo_specs(tm, D_MODEL, SUBLANE) + [
            _mods_spec(layer),
            _layer_spec(g_pre.shape, layer),
            _layer_spec(g_post.shape, layer),
            _layer_spec(w_up.shape, layer, single_buffer=True),
            _layer_spec(conv_w.shape, layer),
            _layer_spec(conv_b.shape, layer),
            _layer_spec(w_down.shape, layer, single_buffer=True),
        ],
        out_specs=pl.BlockSpec((tm, D_MODEL), lambda i: (i, 0)),
        scratch_shapes=[pltpu.VMEM((D_MODEL // LANE, SUBLANE * SLAB_PITCH, LANE), f32),
                        pltpu.VMEM((tm + 2 * SUBLANE, D_MODEL), bf16),
                        pltpu.VMEM((tm, D_FF), bf16)],
        compiler_params=_params(1),
        name="conv_ffn",
    )(x2d, x2d, x2d, mods, g_pre, g_post, w_up, conv_w, conv_b, w_down)


def _gelu_tanh(x):
    return x * jax.nn.sigmoid(x * (2.0 * GELU_C + (2.0 * GELU_C * 0.044715) * (x * x)))


def _odd_mixer_kernel(x_ref, mods_ref, gpre_ref, gpost_ref, win_ref, lng_ref, lnb_ref, ws_ref, bs_ref,
                      wout_ref, o_ref, gated_ref):
    tm = TM_ODD
    nch = tm // CHUNK
    gw = D_MODEL // N_SGU_GROUPS
    b, _, _ = _seq_tile(tm)
    x = x_ref[...]
    h = _pre(x, gpre_ref[...], _mod_row(mods_ref, SH_M, b), _mod_row(mods_ref, SC_M, b)).astype(bf16)
    z = jnp.concatenate(
        [_gelu_tanh(jnp.dot(h, win_ref[:, c0:c0 + ODD_CHUNK], preferred_element_type=f32))
         for c0 in range(0, 2 * D_MODEL, ODD_CHUNK)], axis=1)
    u = z[:, :D_MODEL]
    v = z[:, D_MODEL:]
    mu = jnp.mean(v, axis=-1, keepdims=True)
    vc = v - mu
    var = jnp.mean(vc * vc, axis=-1, keepdims=True)
    vn = (vc * lax.rsqrt(var + EPS) * lng_ref[...] + lnb_ref[...]).astype(bf16)
    for g in range(N_SGU_GROUPS):
        lanes = slice(g * gw, (g + 1) * gw)
        vg = jnp.concatenate([vn[n * CHUNK:(n + 1) * CHUNK, lanes] for n in range(nch)], axis=1)
        s = jnp.dot(ws_ref[g], vg, preferred_element_type=f32)
        for n in range(nch):
            rows = slice(n * CHUNK, (n + 1) * CHUNK)
            gated_ref[rows, lanes] = (u[rows, lanes] * (s[:, n * gw:(n + 1) * gw] + bs_ref[g])).astype(bf16)
    y = jnp.dot(gated_ref[...], wout_ref[...], preferred_element_type=f32)
    o_ref[...] = x + _rms(y, _mod_row(mods_ref, GT_M, b) * gpost_ref[...])


def _odd_mixer(x2d, mods, g_pre, g_post, w_in, ln_g, ln_b, w_s, b_s, w_out):
    tm = TM_ODD
    return pl.pallas_call(
        _odd_mixer_kernel,
        out_shape=jax.ShapeDtypeStruct((N_TOK, D_MODEL), f32),
        grid=(N_TOK // tm,),
        in_specs=[
            pl.BlockSpec((tm, D_MODEL), lambda i: (i, 0)),
            _mods_spec(1),
            _layer_spec(g_pre.shape, 1),
            _layer_spec(g_post.shape, 1),
            _layer_spec(w_in.shape, 0, single_buffer=True),
            _layer_spec(ln_g.shape, 0),
            _layer_spec(ln_b.shape, 0),
            _layer_spec(w_s.shape, 0, single_buffer=True),
            pl.BlockSpec(b_s.shape, lambda i: (0, 0, 0), pipeline_mode=pl.Buffered(1)),
            _layer_spec(w_out.shape, 0, single_buffer=True),
        ],
        out_specs=pl.BlockSpec((tm, D_MODEL), lambda i: (i, 0)),
        scratch_shapes=[pltpu.VMEM((tm, D_MODEL), bf16)],
        compiler_params=_params(1),
        name="odd_mixer",
    )(x2d, mods, g_pre, g_post, w_in, ln_g, ln_b, w_s, b_s, w_out)


def _rope_tables():
    pos = np.arange(SEQ)
    inv = (np.float32(ROPE_BASE) ** (-np.arange(ROPE_FREQS, dtype=np.float32) / ROPE_FREQS)).astype(np.float32)
    ar = (pos // GRID_W).astype(np.float32)[:, None] * inv
    ac = (pos % GRID_W).astype(np.float32)[:, None] * inv
    cos = np.concatenate([np.cos(ar), np.cos(ar), np.cos(ac), np.cos(ac)] * 2, axis=1)
    sin = np.concatenate([-np.sin(ar), np.sin(ar), -np.sin(ac), np.sin(ac)] * 2, axis=1)
    q_scale = HEAD_DIM ** -0.5 * LOG2_E
    return np.stack([cos, sin, cos * q_scale, sin * q_scale]).astype(np.float32)


_ROPE_TABLES = _rope_tables()


def kernel(x, c, ctx, c_ctx, w_ada, b_ada, g_mix_pre, g_mix_post, g_ffn_pre, g_ffn_post, w_in_even, w_pool,
           pool_scale, attn_sink, w_out_even, w_in_odd, sgu_ln_g, sgu_ln_b, sgu_w, sgu_b, w_out_odd,
           w_ffn_up, ffn_conv_w, ffn_conv_b, w_ffn_down):
    assert x.shape == (BATCH, SEQ, D_MODEL) and ctx.shape == (BATCH, CTX_LEN, D_MODEL)
    x2d = x.reshape(N_TOK, D_MODEL)
    ctx2d = ctx.reshape(BATCH * CTX_LEN, D_MODEL)

    cvec = jnp.concatenate([c, c_ctx[None, :], jnp.zeros((8 - BATCH - 1, D_MODEL), f32)], axis=0)
    mods = _adaln(cvec, w_ada, b_ada)

    g_mix_pre, g_mix_post, g_ffn_pre, g_ffn_post = map(_stack_rows, (g_mix_pre, g_mix_post, g_ffn_pre, g_ffn_post))
    w_up = w_ffn_up.astype(bf16)
    w_down = w_ffn_down.astype(bf16)
    conv_b = _stack_rows(ffn_conv_b)

    w_in = w_in_even.astype(bf16)
    kc, vc = _ctx_kv(ctx2d, mods, g_mix_pre, w_in)
    yp, q, k, v = _even_in(x2d, mods, g_mix_pre, w_in, jnp.asarray(_ROPE_TABLES),
                           w_pool.astype(bf16), _stack_rows(pool_scale))
    x2d = _even_attn(attn_sink[0], q, k, v, kc, vc, yp, x2d, mods, g_mix_post, w_out_even.astype(bf16))
    x2d = _conv_ffn(x2d, mods, 0, g_ffn_pre, g_ffn_post, w_up, ffn_conv_w, conv_b, w_down)

    b_s = jnp.broadcast_to(sgu_b[0][:, :, None], (N_SGU_GROUPS, CHUNK, CHUNK))
    x2d = _odd_mixer(x2d, mods, g_mix_pre, g_mix_post, w_in_odd.astype(bf16), _stack_rows(sgu_ln_g),
                     _stack_rows(sgu_ln_b), sgu_w.astype(bf16), b_s, w_out_odd.astype(bf16))
    x2d = _conv_ffn(x2d, mods, 1, g_ffn_pre, g_ffn_post, w_up, ffn_conv_w, conv_b, w_down)
    return x2d.reshape(BATCH, SEQ, D_MODEL)
```

```python
import math

import numpy as np
import jax
import jax.numpy as jnp
from jax import lax
from jax.experimental import pallas as pl
from jax.experimental.pallas import tpu as pltpu

D_MODEL = 1024
BATCH = 4
SEQ = 4096
GRID_W = 64
CTX_LEN = 256
EPS = 1e-6
NEG_INF = -1e30
N_MOD = 6
POOL_WINDOWS = (2, 4, 8, 16)
POOL_GROUP_DIM = 128
POOL_WIDTH = 512
HEAD_DIM = 64
N_Q_HEADS = 8
N_KV_HEADS = 2
GQA_GROUP = N_Q_HEADS // N_KV_HEADS
ATTN_WIDTH = 512
KV_WIDTH = 128
ATTN_BLOCK = 128
ROPE_BASE = 10000.0
ROPE_FREQS = 16
Q_END = POOL_WIDTH + ATTN_WIDTH
CHUNK = 128
N_SGU_GROUPS = 8
D_FF = 2816
LOG2_E = math.log2(math.e)
GELU_C = math.sqrt(2.0 / math.pi)

LANE = 128
SUBLANE = 8
HALO = 16
VMEM_LIMIT = 56 * 1024 * 1024

N_TOK = BATCH * SEQ
TM_IN = 1024
TQ = 1024
TM_FFN = 1024
TM_ODD = 1024
FF_CHUNK = 256
ODD_CHUNK = 512
SLAB_PITCH = TM_FFN // SUBLANE + SUBLANE

f32 = jnp.float32
bf16 = jnp.bfloat16

SH_M, SC_M, GT_M, SH_F, SC_F, GT_F = range(N_MOD)
CTX_ROW = BATCH


def _rms(x, g):
    return x * lax.rsqrt(jnp.mean(x * x, axis=-1, keepdims=True) + EPS) * g


def _pre(x, g, shift, scale):
    return _rms(x, g * (1.0 + scale)) + shift


def _mod_row(mods_ref, slot, row):
    return mods_ref[slot, pl.ds(row, 1), :]


def _layer_spec(shape, layer, single_buffer=False):
    zeros = (0,) * (len(shape) - 1)
    mode = dict(pipeline_mode=pl.Buffered(1)) if single_buffer else {}
    return pl.BlockSpec((None,) + tuple(shape[1:]), lambda *_: (layer,) + zeros, **mode)


def _mods_spec(layer):
    return pl.BlockSpec((None, N_MOD, 8, D_MODEL), lambda *_: (layer, 0, 0, 0))


def _params(n_axes=1):
    return pltpu.CompilerParams(dimension_semantics=("arbitrary",) * n_axes,
                                vmem_limit_bytes=VMEM_LIMIT)


def _stack_rows(a):
    return a.reshape(a.shape[0], 1, a.shape[1])


def _adaln_kernel(c_ref, w_ref, b_ref, o_ref):
    s = jax.nn.silu(c_ref[...]).astype(bf16)
    o_ref[...] = jnp.dot(s, w_ref[...].astype(bf16), preferred_element_type=f32) + b_ref[...]


def _adaln(cvec, w_ada, b_ada):
    depth = w_ada.shape[0]
    return pl.pallas_call(
        _adaln_kernel,
        out_shape=jax.ShapeDtypeStruct((depth, N_MOD, 8, D_MODEL), f32),
        grid=(depth, N_MOD),
        in_specs=[
            pl.BlockSpec((8, D_MODEL), lambda l, j: (0, 0)),
            pl.BlockSpec((None, D_MODEL, D_MODEL), lambda l, j: (l, 0, j)),
            pl.BlockSpec((None, None, 1, D_MODEL), lambda l, j: (l, j, 0, 0)),
        ],
        out_specs=pl.BlockSpec((None, None, 8, D_MODEL), lambda l, j: (l, j, 0, 0)),
        compiler_params=_params(2),
        name="adaln",
    )(cvec, w_ada, b_ada.reshape(depth, N_MOD, 1, D_MODEL))


def _store_kv(k, v, k_ref, v_ref):
    low = lax.broadcasted_iota(jnp.int32, v.shape, 1) < HEAD_DIM
    swapped = pltpu.roll(v, HEAD_DIM, axis=1)
    kt = k.T.astype(bf16)
    for head, vals in enumerate((v, swapped)):
        k_ref[head] = kt[head * HEAD_DIM:(head + 1) * HEAD_DIM]
        v_ref[head] = jnp.where(low, vals, 1.0).astype(bf16)


def _ctx_kv_kernel(x_ref, mods_ref, g_ref, w_ref, k_ref, v_ref):
    h = _pre(x_ref[...], g_ref[...], mods_ref[SH_M, CTX_ROW:CTX_ROW + 1, :],
             mods_ref[SC_M, CTX_ROW:CTX_ROW + 1, :]).astype(bf16)
    z = jnp.dot(h, w_ref[:, Q_END:], preferred_element_type=f32)
    _store_kv(z[:, :KV_WIDTH], z[:, KV_WIDTH:], k_ref, v_ref)


def _ctx_kv(ctx2d, mods, g, w_in):
    n = ctx2d.shape[0]
    return pl.pallas_call(
        _ctx_kv_kernel,
        out_shape=(jax.ShapeDtypeStruct((N_KV_HEADS, HEAD_DIM, n), bf16),
                   jax.ShapeDtypeStruct((N_KV_HEADS, n, LANE), bf16)),
        grid=(1,),
        in_specs=[
            pl.BlockSpec((n, D_MODEL), lambda i: (0, 0)),
            _mods_spec(0),
            _layer_spec(g.shape, 0),
            _layer_spec(w_in.shape, 0),
        ],
        out_specs=(pl.BlockSpec((N_KV_HEADS, HEAD_DIM, n), lambda i: (0, 0, 0)),
                   pl.BlockSpec((N_KV_HEADS, n, LANE), lambda i: (0, 0, 0))),
        compiler_params=_params(1),
        name="ctx_kv",
    )(ctx2d, mods, g, w_in)


def _halo_specs(tm, width, halo=HALO):
    per = tm // halo
    last = N_TOK // halo - 1
    return [
        pl.BlockSpec((tm, width), lambda i: (i, 0)),
        pl.BlockSpec((halo, width), lambda i: (jnp.maximum(i * per - 1, 0), 0)),
        pl.BlockSpec((halo, width), lambda i: (jnp.minimum((i + 1) * per, last), 0)),
    ]


def _seq_tile(tm):
    tps = SEQ // tm
    i = pl.program_id(0)
    return i // tps, i % tps, tps


def _normed_with_halo(xm_ref, xp_ref, xn_ref, g, shift, scale, tm):
    _, ti, tps = _seq_tile(tm)
    hp = _pre(xp_ref[...], g, shift, scale)
    hn = _pre(xn_ref[...], g, shift, scale)
    hp = jnp.where(ti > 0, hp, 0.0)
    hn = jnp.where(ti < tps - 1, hn, 0.0)
    hm = _pre(xm_ref[...], g, shift, scale)
    return jnp.concatenate([hp, hm, hn], axis=0).astype(bf16)


def _rope(z, cos, sin_signed, lane):
    fwd = pltpu.roll(z, LANE - ROPE_FREQS, axis=1)
    bwd = pltpu.roll(z, ROPE_FREQS, axis=1)
    partner = jnp.where((lane % (2 * ROPE_FREQS)) < ROPE_FREQS, fwd, bwd)
    return z * cos + partner * sin_signed


def _even_in_kernel(xm_ref, xp_ref, xn_ref, mods_ref, g_ref, w_ref, rope_ref,
                    wpool_ref, pscale_ref, yp_ref, q_ref, k_ref, v_ref):
    tm = TM_IN
    b, ti, _ = _seq_tile(tm)
    shift = _mod_row(mods_ref, SH_M, b)
    scale = _mod_row(mods_ref, SC_M, b)
    h = _normed_with_halo(xm_ref, xp_ref, xn_ref, g_ref[...], shift, scale, tm)
    z = jnp.dot(h, w_ref[...], preferred_element_type=f32)

    zm = z[HALO:HALO + tm]
    lane = lax.broadcasted_iota(jnp.int32, (tm, LANE), 1)
    for j in range(ATTN_WIDTH // LANE):
        c0 = POOL_WIDTH + j * LANE
        pair = _rope(zm[:, c0:c0 + LANE], rope_ref[2], rope_ref[3], lane)
        q_ref[2 * j] = pair[:, :HEAD_DIM].astype(bf16)
        q_ref[2 * j + 1] = pair[:, HEAD_DIM:].astype(bf16)
    _store_kv(_rope(zm[:, Q_END:Q_END + KV_WIDTH], rope_ref[0], rope_ref[1], lane),
              zm[:, Q_END + KV_WIDTH:], k_ref, v_ref)

    rows = tm + 2 * HALO
    edge = lax.broadcasted_iota(jnp.int32, (SUBLANE, POOL_GROUP_DIM), 0) + ti * tm
    for gi, win in enumerate(POOL_WINDOWS):
        lanes = slice(gi * POOL_GROUP_DIM, (gi + 1) * POOL_GROUP_DIM)
        u = z[:, lanes]
        s = u + pltpu.roll(u, 1, axis=0)
        step = 1
        while 2 * step < win:
            s = pltpu.roll(s, step, axis=0) + pltpu.roll(s, rows - step, axis=0)
            step *= 2
        half = win // 2
        count = lambda pos: (jnp.minimum(pos + half, SEQ) - jnp.maximum(pos - half, 0)).astype(f32)
        sm = s[HALO:HALO + tm]
        mean = jnp.concatenate([sm[:SUBLANE] / count(edge),
                                sm[SUBLANE:tm - SUBLANE] * (1.0 / win),
                                sm[tm - SUBLANE:] / count(edge + (tm - SUBLANE))], axis=0)
        pooled = (mean - u[HALO:HALO + tm]).astype(bf16)
        y = jnp.dot(pooled, wpool_ref[gi], preferred_element_type=f32)
        yp_ref[:, lanes] = (y * pscale_ref[:, lanes]).astype(bf16)


def _even_in(x2d, mods, g, w_in, rope, w_pool, pool_scale):
    tm = TM_IN
    tps = SEQ // tm
    heads = lambda n, w: pl.BlockSpec((n, tm, w), lambda i: (0, i, 0))
    return pl.pallas_call(
        _even_in_kernel,
        out_shape=(jax.ShapeDtypeStruct((N_TOK, POOL_WIDTH), bf16),
                   jax.ShapeDtypeStruct((N_Q_HEADS, N_TOK, HEAD_DIM), bf16),
                   jax.ShapeDtypeStruct((N_KV_HEADS, HEAD_DIM, N_TOK), bf16),
                   jax.ShapeDtypeStruct((N_KV_HEADS, N_TOK, LANE), bf16)),
        grid=(N_TOK // tm,),
        in_specs=_halo_specs(tm, D_MODEL) + [
            _mods_spec(0),
            _layer_spec(g.shape, 0),
            _layer_spec(w_in.shape, 0, single_buffer=True),
            pl.BlockSpec((rope.shape[0], tm, LANE), lambda i: (0, i % tps, 0)),
            _layer_spec(w_pool.shape, 0, single_buffer=True),
            _layer_spec(pool_scale.shape, 0),
        ],
        out_specs=(pl.BlockSpec((tm, POOL_WIDTH), lambda i: (i, 0)), heads(N_Q_HEADS, HEAD_DIM),
                   pl.BlockSpec((N_KV_HEADS, HEAD_DIM, tm), lambda i: (0, 0, i)), heads(N_KV_HEADS, LANE)),
        compiler_params=_params(1),
        name="even_in",
    )(x2d, x2d, x2d, mods, g, w_in, rope, w_pool, pool_scale)


def _even_attn_kernel(sink_ref, q_ref, km_ref, kp_ref, kn_ref, vm_ref, vp_ref, vn_ref,
                      kc_ref, vc_ref, yp_ref, x_ref, mods_ref, g_ref, wout_ref, *rest):
    n_cast = (len(rest) - 2) // 2
    cast_in, o_ref, cast_out, mix_ref = rest[:n_cast], rest[n_cast], rest[n_cast + 1:-1], rest[-1]
    for src, dst in zip(cast_in, cast_out):
        dst[...] = src[...].astype(bf16)
    b, ti, tps = _seq_tile(TQ)
    nblk = TQ // ATTN_BLOCK

    def row_blocks(main_ref, prev_ref, next_ref, ctx_ref, g):
        blocks = [prev_ref[g]]
        blocks += [main_ref[g, j * ATTN_BLOCK:(j + 1) * ATTN_BLOCK, :] for j in range(nblk)]
        return blocks + [next_ref[g], ctx_ref[g]]

    ri = lax.broadcasted_iota(jnp.int32, (ATTN_BLOCK, ATTN_BLOCK), 0)
    ci = lax.broadcasted_iota(jnp.int32, (ATTN_BLOCK, ATTN_BLOCK), 1)
    lower_ok = ci >= ri
    upper_ok = ci <= ri
    first_ok = jnp.logical_and(lower_ok, ti > 0)
    last_ok = jnp.logical_and(upper_ok, ti < tps - 1)
    low = lax.broadcasted_iota(jnp.int32, (ATTN_BLOCK, LANE), 1) < HEAD_DIM

    mix_ref[:, :POOL_WIDTH] = yp_ref[...]
    for g in range(N_KV_HEADS):
        kblocks = ([kp_ref[g]] + [km_ref[g, :, j * ATTN_BLOCK:(j + 1) * ATTN_BLOCK] for j in range(nblk)]
                   + [kn_ref[g], kc_ref[g]])
        vblocks = row_blocks(vm_ref, vp_ref, vn_ref, vc_ref, g)
        for j in range(nblk):
            sel = [j, j + 1, j + 2, nblk + 2]
            rows = slice(j * ATTN_BLOCK, (j + 1) * ATTN_BLOCK)
            keys = jnp.concatenate([kblocks[s] for s in sel], axis=1)
            vals = jnp.concatenate([vblocks[s] for s in sel], axis=0)
            qg = jnp.concatenate([q_ref[GQA_GROUP * g + hh, rows, :] for hh in range(GQA_GROUP)], axis=0)
            s_all = jnp.dot(qg, keys, preferred_element_type=f32)
            m_prev = first_ok if j == 0 else lower_ok
            m_next = last_ok if j == nblk - 1 else upper_ok
            p_rows = []
            sink_terms = []
            for hh in range(GQA_GROUP):
                sink = sink_ref[GQA_GROUP * g + hh] * LOG2_E
                s = s_all[hh * ATTN_BLOCK:(hh + 1) * ATTN_BLOCK]
                s = jnp.concatenate([
                    jnp.where(m_prev, s[:, :ATTN_BLOCK], NEG_INF),
                    s[:, ATTN_BLOCK:2 * ATTN_BLOCK],
                    jnp.where(m_next, s[:, 2 * ATTN_BLOCK:3 * ATTN_BLOCK], NEG_INF),
                    s[:, 3 * ATTN_BLOCK:]], axis=1)
                m = jnp.maximum(jnp.max(s, axis=-1, keepdims=True), sink)
                p_rows.append(jnp.exp2(s - m).astype(bf16))
                sink_terms.append(jnp.exp2(sink - m))
            out = jnp.dot(jnp.concatenate(p_rows, axis=0), vals, preferred_element_type=f32)
            for pr in range(GQA_GROUP // 2):
                even = out[(2 * pr) * ATTN_BLOCK:(2 * pr + 1) * ATTN_BLOCK]
                odd = out[(2 * pr + 1) * ATTN_BLOCK:(2 * pr + 2) * ATTN_BLOCK]
                o = jnp.where(low, even / (pltpu.roll(even, HEAD_DIM, axis=1) + sink_terms[2 * pr]),
                              pltpu.roll(odd, HEAD_DIM, axis=1) / (odd + sink_terms[2 * pr + 1]))
                c0 = POOL_WIDTH + (2 * g + pr) * LANE
                mix_ref[rows, c0:c0 + LANE] = o.astype(bf16)

    y = jnp.dot(mix_ref[...], wout_ref[...], preferred_element_type=f32)
    o_ref[...] = x_ref[...] + _rms(y, _mod_row(mods_ref, GT_M, b) * g_ref[...])


def _even_attn(sink, q, k, v, kc, vc, yp, x2d, mods, g_post, w_out, later_weights):
    nsteps = N_TOK // TQ
    flat = [w.reshape(-1, w.shape[-1]) for w in later_weights]
    cast_specs = [pl.BlockSpec((w.shape[0] // nsteps, w.shape[1]), lambda i: (i, 0)) for w in flat]
    per = TQ // ATTN_BLOCK
    last = N_TOK // ATTN_BLOCK - 1
    tps = SEQ // TQ
    prev_blk = lambda i: jnp.maximum(i * per - 1, 0)
    next_blk = lambda i: jnp.minimum((i + 1) * per, last)
    k_band = [
        pl.BlockSpec((N_KV_HEADS, HEAD_DIM, TQ), lambda i: (0, 0, i)),
        pl.BlockSpec((N_KV_HEADS, HEAD_DIM, ATTN_BLOCK), lambda i: (0, 0, prev_blk(i))),
        pl.BlockSpec((N_KV_HEADS, HEAD_DIM, ATTN_BLOCK), lambda i: (0, 0, next_blk(i))),
    ]
    v_band = [
        pl.BlockSpec((N_KV_HEADS, TQ, LANE), lambda i: (0, i, 0)),
        pl.BlockSpec((N_KV_HEADS, ATTN_BLOCK, LANE), lambda i: (0, prev_blk(i), 0)),
        pl.BlockSpec((N_KV_HEADS, ATTN_BLOCK, LANE), lambda i: (0, next_blk(i), 0)),
    ]
    outs = pl.pallas_call(
        _even_attn_kernel,
        out_shape=[jax.ShapeDtypeStruct((N_TOK, D_MODEL), f32)]
                  + [jax.ShapeDtypeStruct(w.shape, bf16) for w in flat],
        grid=(nsteps,),
        in_specs=[pl.BlockSpec(memory_space=pltpu.SMEM),
                  pl.BlockSpec((N_Q_HEADS, TQ, HEAD_DIM), lambda i: (0, i, 0))]
                 + k_band + v_band + [
            pl.BlockSpec((N_KV_HEADS, HEAD_DIM, CTX_LEN), lambda i: (0, 0, i // tps)),
            pl.BlockSpec((N_KV_HEADS, CTX_LEN, LANE), lambda i: (0, i // tps, 0)),
            pl.BlockSpec((TQ, POOL_WIDTH), lambda i: (i, 0)),
            pl.BlockSpec((TQ, D_MODEL), lambda i: (i, 0)),
            _mods_spec(0),
            _layer_spec(g_post.shape, 0),
            _layer_spec(w_out.shape, 0, single_buffer=True),
        ] + cast_specs,
        out_specs=[pl.BlockSpec((TQ, D_MODEL), lambda i: (i, 0))] + cast_specs,
        scratch_shapes=[pltpu.VMEM((TQ, POOL_WIDTH + ATTN_WIDTH), bf16)],
        compiler_params=_params(1),
        name="even_attn",
    )(sink, q, k, k, k, v, v, v, kc, vc, yp, x2d, mods, g_post, w_out, *flat)
    return outs[0], [o.reshape(w.shape) for o, w in zip(outs[1:], later_weights)]


def _conv_ffn_kernel(xm_ref, xp_ref, xn_ref, mods_ref, gpre_ref, gpost_ref, wup_ref, cw_ref, cb_ref,
                     wdn_ref, o_ref, slab_ref, h_ref, act_ref):
    tm = TM_FFN
    nr = tm // SUBLANE
    nslab = D_MODEL // LANE
    b, ti, tps = _seq_tile(tm)
    sub = lax.broadcasted_iota(jnp.int32, (SUBLANE, FF_CHUNK), 0)

    def conv(hu, c0):
        w = cw_ref[:, c0:c0 + FF_CHUNK]
        first = jnp.where(sub == 0, pltpu.roll(hu[tm:tm + SUBLANE], 1, axis=0),
                          pltpu.roll(hu[tm - SUBLANE:tm], 1, axis=0))
        last = jnp.where(sub == SUBLANE - 1, pltpu.roll(hu[tm + SUBLANE:], SUBLANE - 1, axis=0),
                         pltpu.roll(hu[:SUBLANE], SUBLANE - 1, axis=0))
        before = jnp.concatenate([first, hu[:tm - SUBLANE]], axis=0)
        after = jnp.concatenate([hu[SUBLANE:tm], last], axis=0)
        return before * w[0:1] + hu[:tm] * w[1:2] + after * w[2:3] + cb_ref[:, c0:c0 + FF_CHUNK]

    g = gpre_ref[...]
    shift = _mod_row(mods_ref, SH_F, b)
    scale = _mod_row(mods_ref, SC_F, b)
    val = _pre(xm_ref[...], g, shift, scale)
    for j in range(nslab):
        for a in range(SUBLANE):
            slab_ref[j, a * SLAB_PITCH:a * SLAB_PITCH + nr, :] = val[a * nr:(a + 1) * nr, j * LANE:(j + 1) * LANE]
    for r in range(0, nr, 2):
        rows = [jnp.concatenate([slab_ref[j, pl.ds(r + k, SUBLANE, stride=SLAB_PITCH), :]
                                 for j in range(nslab)], axis=1) for k in range(2)]
        h_ref[r * SUBLANE:(r + 2) * SUBLANE, :] = jnp.concatenate(rows, axis=0).astype(bf16)
    hp = jnp.where(ti > 0, _pre(xp_ref[...], g, shift, scale), 0.0)
    hn = jnp.where(ti < tps - 1, _pre(xn_ref[...], g, shift, scale), 0.0)
    h_ref[tm:, :] = jnp.concatenate([hp, hn], axis=0).astype(bf16)
    h = h_ref[...]
    for c in range(D_FF // FF_CHUNK):
        g0 = c * FF_CHUNK
        u0 = D_FF + c * FF_CHUNK
        gate = conv(jnp.dot(h, wup_ref[:, g0:g0 + FF_CHUNK], preferred_element_type=f32), g0)
        up = conv(jnp.dot(h, wup_ref[:, u0:u0 + FF_CHUNK], preferred_element_type=f32), u0)
        act_ref[:, g0:g0 + FF_CHUNK] = (jax.nn.silu(gate) * up).astype(bf16)

    y = jnp.dot(act_ref[...], wdn_ref[...], preferred_element_type=f32)
    res = _rms(y, _mod_row(mods_ref, GT_F, b) * gpost_ref[...])
    for r in range(nr):
        for j in range(nslab):
            slab_ref[j, pl.ds(r, SUBLANE, stride=SLAB_PITCH), :] = (
                res[r * SUBLANE:(r + 1) * SUBLANE, j * LANE:(j + 1) * LANE])
    for j in range(nslab):
        for a in range(SUBLANE):
            rows = slice(a * nr, (a + 1) * nr)
            lanes = slice(j * LANE, (j + 1) * LANE)
            o_ref[rows, lanes] = xm_ref[rows, lanes] + slab_ref[j, a * SLAB_PITCH:a * SLAB_PITCH + nr, :]


def _conv_ffn(x2d, mods, layer, g_pre, g_post, w_up, conv_w, conv_b, w_down):
    tm = TM_FFN
    return pl.pallas_call(
        _conv_ffn_kernel,
        out_shape=jax.ShapeDtypeStruct((N_TOK, D_MODEL), f32),
        grid=(N_TOK // tm,),
        in_specs=_halo_specs(tm, D_MODEL, SUBLANE) + [
            _mods_spec(layer),
            _layer_spec(g_pre.shape, layer),
            _layer_spec(g_post.shape, layer),
            _layer_spec(w_up.shape, layer, single_buffer=True),
            _layer_spec(conv_w.shape, layer),
            _layer_spec(conv_b.shape, layer),
            _layer_spec(w_down.shape, layer, single_buffer=True),
        ],
        out_specs=pl.BlockSpec((tm, D_MODEL), lambda i: (i, 0)),
        scratch_shapes=[pltpu.VMEM((D_MODEL // LANE, SUBLANE * SLAB_PITCH, LANE), f32),
                        pltpu.VMEM((tm + 2 * SUBLANE, D_MODEL), bf16),
                        pltpu.VMEM((tm, D_FF), bf16)],
        compiler_params=_params(1),
        name="conv_ffn",
    )(x2d, x2d, x2d, mods, g_pre, g_post, w_up, conv_w, conv_b, w_down)


def _gelu_tanh(x):
    return x * jax.nn.sigmoid(x * (2.0 * GELU_C + (2.0 * GELU_C * 0.044715) * (x * x)))


def _odd_mixer_kernel(x_ref, mods_ref, gpre_ref, gpost_ref, win_ref, lng_ref, lnb_ref, ws_ref, bs_ref,
                      wout_ref, o_ref, gated_ref):
    tm = TM_ODD
    nch = tm // CHUNK
    gw = D_MODEL // N_SGU_GROUPS
    b, _, _ = _seq_tile(tm)
    x = x_ref[...]
    h = _pre(x, gpre_ref[...], _mod_row(mods_ref, SH_M, b), _mod_row(mods_ref, SC_M, b)).astype(bf16)
    z = jnp.concatenate(
        [_gelu_tanh(jnp.dot(h, win_ref[:, c0:c0 + ODD_CHUNK], preferred_element_type=f32))
         for c0 in range(0, 2 * D_MODEL, ODD_CHUNK)], axis=1)
    u = z[:, :D_MODEL]
    v = z[:, D_MODEL:]
    mu = jnp.mean(v, axis=-1, keepdims=True)
    vc = v - mu
    var = jnp.mean(vc * vc, axis=-1, keepdims=True)
    vn = (vc * lax.rsqrt(var + EPS) * lng_ref[...] + lnb_ref[...]).astype(bf16)
    for g in range(N_SGU_GROUPS):
        lanes = slice(g * gw, (g + 1) * gw)
        vg = jnp.concatenate([vn[n * CHUNK:(n + 1) * CHUNK, lanes] for n in range(nch)], axis=1)
        s = jnp.dot(ws_ref[g], vg, preferred_element_type=f32)
        for n in range(nch):
            rows = slice(n * CHUNK, (n + 1) * CHUNK)
            gated_ref[rows, lanes] = (u[rows, lanes] * (s[:, n * gw:(n + 1) * gw] + bs_ref[g])).astype(bf16)
    y = jnp.dot(gated_ref[...], wout_ref[...], preferred_element_type=f32)
    o_ref[...] = x + _rms(y, _mod_row(mods_ref, GT_M, b) * gpost_ref[...])


def _odd_mixer(x2d, mods, g_pre, g_post, w_in, ln_g, ln_b, w_s, b_s, w_out):
    tm = TM_ODD
    return pl.pallas_call(
        _odd_mixer_kernel,
        out_shape=jax.ShapeDtypeStruct((N_TOK, D_MODEL), f32),
        grid=(N_TOK // tm,),
        in_specs=[
            pl.BlockSpec((tm, D_MODEL), lambda i: (i, 0)),
            _mods_spec(1),
            _layer_spec(g_pre.shape, 1),
            _layer_spec(g_post.shape, 1),
            _layer_spec(w_in.shape, 0, single_buffer=True),
            _layer_spec(ln_g.shape, 0),
            _layer_spec(ln_b.shape, 0),
            _layer_spec(w_s.shape, 0, single_buffer=True),
            pl.BlockSpec(b_s.shape, lambda i: (0, 0, 0), pipeline_mode=pl.Buffered(1)),
            _layer_spec(w_out.shape, 0, single_buffer=True),
        ],
        out_specs=pl.BlockSpec((tm, D_MODEL), lambda i: (i, 0)),
        scratch_shapes=[pltpu.VMEM((tm, D_MODEL), bf16)],
        compiler_params=_params(1),
        name="odd_mixer",
    )(x2d, mods, g_pre, g_post, w_in, ln_g, ln_b, w_s, b_s, w_out)


def _rope_tables():
    pos = np.arange(SEQ)
    inv = (np.float32(ROPE_BASE) ** (-np.arange(ROPE_FREQS, dtype=np.float32) / ROPE_FREQS)).astype(np.float32)
    ar = (pos // GRID_W).astype(np.float32)[:, None] * inv
    ac = (pos % GRID_W).astype(np.float32)[:, None] * inv
    cos = np.concatenate([np.cos(ar), np.cos(ar), np.cos(ac), np.cos(ac)] * 2, axis=1)
    sin = np.concatenate([-np.sin(ar), np.sin(ar), -np.sin(ac), np.sin(ac)] * 2, axis=1)
    q_scale = HEAD_DIM ** -0.5 * LOG2_E
    return np.stack([cos, sin, cos * q_scale, sin * q_scale]).astype(np.float32)


_ROPE_TABLES = _rope_tables()


def kernel(x, c, ctx, c_ctx, w_ada, b_ada, g_mix_pre, g_mix_post, g_ffn_pre, g_ffn_post, w_in_even, w_pool,
           pool_scale, attn_sink, w_out_even, w_in_odd, sgu_ln_g, sgu_ln_b, sgu_w, sgu_b, w_out_odd,
           w_ffn_up, ffn_conv_w, ffn_conv_b, w_ffn_down):
    assert x.shape == (BATCH, SEQ, D_MODEL) and ctx.shape == (BATCH, CTX_LEN, D_MODEL)
    x2d = x.reshape(N_TOK, D_MODEL)
    ctx2d = ctx.reshape(BATCH * CTX_LEN, D_MODEL)

    cvec = jnp.concatenate([c, c_ctx[None, :], jnp.zeros((8 - BATCH - 1, D_MODEL), f32)], axis=0)
    mods = _adaln(cvec, w_ada, b_ada)

    g_mix_pre, g_mix_post, g_ffn_pre, g_ffn_post = map(_stack_rows, (g_mix_pre, g_mix_post, g_ffn_pre, g_ffn_post))
    conv_b = _stack_rows(ffn_conv_b)

    w_in = w_in_even.astype(bf16)
    kc, vc = _ctx_kv(ctx2d, mods, g_mix_pre, w_in)
    yp, q, k, v = _even_in(x2d, mods, g_mix_pre, w_in, jnp.asarray(_ROPE_TABLES),
                           w_pool.astype(bf16), _stack_rows(pool_scale))
    x2d, (w_up, w_down, w_in_o, w_out_o, w_sgu) = _even_attn(
        attn_sink[0], q, k, v, kc, vc, yp, x2d, mods, g_mix_post, w_out_even.astype(bf16),
        [w_ffn_up, w_ffn_down, w_in_odd, w_out_odd, sgu_w])
    x2d = _conv_ffn(x2d, mods, 0, g_ffn_pre, g_ffn_post, w_up, ffn_conv_w, conv_b, w_down)

    b_s = jnp.broadcast_to(sgu_b[0][:, :, None], (N_SGU_GROUPS, CHUNK, CHUNK))
    x2d = _odd_mixer(x2d, mods, g_mix_pre, g_mix_post, w_in_o, _stack_rows(sgu_ln_g),
                     _stack_rows(sgu_ln_b), w_sgu, b_s, w_out_o)
    x2d = _conv_ffn(x2d, mods, 1, g_ffn_pre, g_ffn_post, w_up, ffn_conv_w, conv_b, w_down)
    return x2d.reshape(BATCH, SEQ, D_MODEL)
```

```python
import math

import numpy as np
import jax
import jax.numpy as jnp
from jax import lax
from jax.experimental import pallas as pl
from jax.experimental.pallas import tpu as pltpu

D_MODEL = 1024
BATCH = 4
SEQ = 4096
GRID_W = 64
CTX_LEN = 256
EPS = 1e-6
NEG_INF = -1e30
N_MOD = 6
POOL_WINDOWS = (2, 4, 8, 16)
POOL_GROUP_DIM = 128
POOL_WIDTH = 512
HEAD_DIM = 64
N_Q_HEADS = 8
N_KV_HEADS = 2
GQA_GROUP = N_Q_HEADS // N_KV_HEADS
ATTN_WIDTH = 512
KV_WIDTH = 128
ATTN_BLOCK = 128
ROPE_BASE = 10000.0
ROPE_FREQS = 16
Q_END = POOL_WIDTH + ATTN_WIDTH
CHUNK = 128
N_SGU_GROUPS = 8
D_FF = 2816
LOG2_E = math.log2(math.e)
GELU_C = math.sqrt(2.0 / math.pi)

LANE = 128
SUBLANE = 8
HALO = 16
VMEM_LIMIT = 56 * 1024 * 1024

N_TOK = BATCH * SEQ
TM_IN = 1024
TQ = 1024
TM_FFN = 1024
TM_ODD = 1024
FF_CHUNK = 256
FFN_DOWN_PARTS = 2
ODD_CHUNK = 512
SLAB_PITCH = TM_FFN // SUBLANE + SUBLANE

f32 = jnp.float32
bf16 = jnp.bfloat16

SH_M, SC_M, GT_M, SH_F, SC_F, GT_F = range(N_MOD)
CTX_ROW = BATCH


def _rms(x, g):
    return x * lax.rsqrt(jnp.mean(x * x, axis=-1, keepdims=True) + EPS) * g


def _pre(x, g, shift, scale):
    return _rms(x, g * (1.0 + scale)) + shift


def _mod_row(mods_ref, slot, row):
    return mods_ref[slot, pl.ds(row, 1), :]


def _layer_spec(shape, layer, single_buffer=False):
    zeros = (0,) * (len(shape) - 1)
    mode = dict(pipeline_mode=pl.Buffered(1)) if single_buffer else {}
    return pl.BlockSpec((None,) + tuple(shape[1:]), lambda *_: (layer,) + zeros, **mode)


def _mods_spec(layer):
    return pl.BlockSpec((None, N_MOD, 8, D_MODEL), lambda *_: (layer, 0, 0, 0))


def _params(n_axes=1):
    return pltpu.CompilerParams(dimension_semantics=("arbitrary",) * n_axes,
                                vmem_limit_bytes=VMEM_LIMIT)


def _stack_rows(a):
    return a.reshape(a.shape[0], 1, a.shape[1])


def _adaln_kernel(c_ref, w_ref, b_ref, o_ref):
    s = jax.nn.silu(c_ref[...]).astype(bf16)
    o_ref[...] = jnp.dot(s, w_ref[...].astype(bf16), preferred_element_type=f32) + b_ref[...]


def _adaln(cvec, w_ada, b_ada):
    depth = w_ada.shape[0]
    return pl.pallas_call(
        _adaln_kernel,
        out_shape=jax.ShapeDtypeStruct((depth, N_MOD, 8, D_MODEL), f32),
        grid=(depth, N_MOD),
        in_specs=[
            pl.BlockSpec((8, D_MODEL), lambda l, j: (0, 0)),
            pl.BlockSpec((None, D_MODEL, D_MODEL), lambda l, j: (l, 0, j)),
            pl.BlockSpec((None, None, 1, D_MODEL), lambda l, j: (l, j, 0, 0)),
        ],
        out_specs=pl.BlockSpec((None, None, 8, D_MODEL), lambda l, j: (l, j, 0, 0)),
        compiler_params=_params(2),
        name="adaln",
    )(cvec, w_ada, b_ada.reshape(depth, N_MOD, 1, D_MODEL))


def _store_kv(k, v, k_ref, v_ref):
    low = lax.broadcasted_iota(jnp.int32, v.shape, 1) < HEAD_DIM
    swapped = pltpu.roll(v, HEAD_DIM, axis=1)
    kt = k.T.astype(bf16)
    for head, vals in enumerate((v, swapped)):
        k_ref[head] = kt[head * HEAD_DIM:(head + 1) * HEAD_DIM]
        v_ref[head] = jnp.where(low, vals, 1.0).astype(bf16)


def _ctx_kv_kernel(x_ref, mods_ref, g_ref, w_ref, k_ref, v_ref):
    h = _pre(x_ref[...], g_ref[...], mods_ref[SH_M, CTX_ROW:CTX_ROW + 1, :],
             mods_ref[SC_M, CTX_ROW:CTX_ROW + 1, :]).astype(bf16)
    z = jnp.dot(h, w_ref[:, Q_END:], preferred_element_type=f32)
    _store_kv(z[:, :KV_WIDTH], z[:, KV_WIDTH:], k_ref, v_ref)


def _ctx_kv(ctx2d, mods, g, w_in):
    n = ctx2d.shape[0]
    return pl.pallas_call(
        _ctx_kv_kernel,
        out_shape=(jax.ShapeDtypeStruct((N_KV_HEADS, HEAD_DIM, n), bf16),
                   jax.ShapeDtypeStruct((N_KV_HEADS, n, LANE), bf16)),
        grid=(1,),
        in_specs=[
            pl.BlockSpec((n, D_MODEL), lambda i: (0, 0)),
            _mods_spec(0),
            _layer_spec(g.shape, 0),
            _layer_spec(w_in.shape, 0),
        ],
        out_specs=(pl.BlockSpec((N_KV_HEADS, HEAD_DIM, n), lambda i: (0, 0, 0)),
                   pl.BlockSpec((N_KV_HEADS, n, LANE), lambda i: (0, 0, 0))),
        compiler_params=_params(1),
        name="ctx_kv",
    )(ctx2d, mods, g, w_in)


def _halo_specs(tm, width, halo=HALO):
    per = tm // halo
    last = N_TOK // halo - 1
    return [
        pl.BlockSpec((tm, width), lambda i: (i, 0)),
        pl.BlockSpec((halo, width), lambda i: (jnp.maximum(i * per - 1, 0), 0)),
        pl.BlockSpec((halo, width), lambda i: (jnp.minimum((i + 1) * per, last), 0)),
    ]


def _seq_tile(tm):
    tps = SEQ // tm
    i = pl.program_id(0)
    return i // tps, i % tps, tps


def _normed_with_halo(xm_ref, xp_ref, xn_ref, g, shift, scale, tm):
    _, ti, tps = _seq_tile(tm)
    hp = _pre(xp_ref[...], g, shift, scale)
    hn = _pre(xn_ref[...], g, shift, scale)
    hp = jnp.where(ti > 0, hp, 0.0)
    hn = jnp.where(ti < tps - 1, hn, 0.0)
    hm = _pre(xm_ref[...], g, shift, scale)
    return jnp.concatenate([hp, hm, hn], axis=0).astype(bf16)


def _rope(z, cos, sin_signed, lane):
    fwd = pltpu.roll(z, LANE - ROPE_FREQS, axis=1)
    bwd = pltpu.roll(z, ROPE_FREQS, axis=1)
    partner = jnp.where((lane % (2 * ROPE_FREQS)) < ROPE_FREQS, fwd, bwd)
    return z * cos + partner * sin_signed


def _even_in_kernel(xm_ref, xp_ref, xn_ref, mods_ref, g_ref, w_ref, rope_ref,
                    wpool_ref, pscale_ref, yp_ref, q_ref, k_ref, v_ref):
    tm = TM_IN
    b, ti, _ = _seq_tile(tm)
    shift = _mod_row(mods_ref, SH_M, b)
    scale = _mod_row(mods_ref, SC_M, b)
    h = _normed_with_halo(xm_ref, xp_ref, xn_ref, g_ref[...], shift, scale, tm)
    hm = h[HALO:HALO + tm]
    z = jnp.dot(h, w_ref[:, :POOL_WIDTH], preferred_element_type=f32)
    zq = jnp.dot(hm, w_ref[:, POOL_WIDTH:Q_END], preferred_element_type=f32)
    zkv = jnp.dot(hm, w_ref[:, Q_END:], preferred_element_type=f32)

    rows = tm + 2 * HALO
    edge = lax.broadcasted_iota(jnp.int32, (SUBLANE, POOL_GROUP_DIM), 0) + ti * tm
    for gi, win in enumerate(POOL_WINDOWS):
        lanes = slice(gi * POOL_GROUP_DIM, (gi + 1) * POOL_GROUP_DIM)
        u = z[:, lanes]
        s = u + pltpu.roll(u, 1, axis=0)
        step = 1
        while 2 * step < win:
            s = pltpu.roll(s, step, axis=0) + pltpu.roll(s, rows - step, axis=0)
            step *= 2
        half = win // 2
        count = lambda pos: (jnp.minimum(pos + half, SEQ) - jnp.maximum(pos - half, 0)).astype(f32)
        sm = s[HALO:HALO + tm]
        mean = jnp.concatenate([sm[:SUBLANE] / count(edge),
                                sm[SUBLANE:tm - SUBLANE] * (1.0 / win),
                                sm[tm - SUBLANE:] / count(edge + (tm - SUBLANE))], axis=0)
        pooled = (mean - u[HALO:HALO + tm]).astype(bf16)
        y = jnp.dot(pooled, wpool_ref[gi], preferred_element_type=f32)
        yp_ref[:, lanes] = (y * pscale_ref[:, lanes]).astype(bf16)

    lane = lax.broadcasted_iota(jnp.int32, (tm, LANE), 1)
    for j in range(ATTN_WIDTH // LANE):
        pair = _rope(zq[:, j * LANE:(j + 1) * LANE], rope_ref[2], rope_ref[3], lane)
        q_ref[2 * j] = pair[:, :HEAD_DIM].astype(bf16)
        q_ref[2 * j + 1] = pair[:, HEAD_DIM:].astype(bf16)
    _store_kv(_rope(zkv[:, :KV_WIDTH], rope_ref[0], rope_ref[1], lane), zkv[:, KV_WIDTH:], k_ref, v_ref)


def _even_in(x2d, mods, g, w_in, rope, w_pool, pool_scale):
    tm = TM_IN
    tps = SEQ // tm
    heads = lambda n, w: pl.BlockSpec((n, tm, w), lambda i: (0, i, 0))
    return pl.pallas_call(
        _even_in_kernel,
        out_shape=(jax.ShapeDtypeStruct((N_TOK, POOL_WIDTH), bf16),
                   jax.ShapeDtypeStruct((N_Q_HEADS, N_TOK, HEAD_DIM), bf16),
                   jax.ShapeDtypeStruct((N_KV_HEADS, HEAD_DIM, N_TOK), bf16),
                   jax.ShapeDtypeStruct((N_KV_HEADS, N_TOK, LANE), bf16)),
        grid=(N_TOK // tm,),
        in_specs=_halo_specs(tm, D_MODEL) + [
            _mods_spec(0),
            _layer_spec(g.shape, 0),
            _layer_spec(w_in.shape, 0, single_buffer=True),
            pl.BlockSpec((rope.shape[0], tm, LANE), lambda i: (0, i % tps, 0)),
            _layer_spec(w_pool.shape, 0, single_buffer=True),
            _layer_spec(pool_scale.shape, 0),
        ],
        out_specs=(pl.BlockSpec((tm, POOL_WIDTH), lambda i: (i, 0)), heads(N_Q_HEADS, HEAD_DIM),
                   pl.BlockSpec((N_KV_HEADS, HEAD_DIM, tm), lambda i: (0, 0, i)), heads(N_KV_HEADS, LANE)),
        compiler_params=_params(1),
        name="even_in",
    )(x2d, x2d, x2d, mods, g, w_in, rope, w_pool, pool_scale)


def _even_attn_kernel(sink_ref, q_ref, km_ref, kp_ref, kn_ref, vm_ref, vp_ref, vn_ref,
                      kc_ref, vc_ref, yp_ref, x_ref, mods_ref, g_ref, wout_ref, *rest):
    n_cast = (len(rest) - 2) // 2
    cast_in, o_ref, cast_out, mix_ref = rest[:n_cast], rest[n_cast], rest[n_cast + 1:-1], rest[-1]
    for src, dst in zip(cast_in, cast_out):
        dst[...] = src[...].astype(bf16)
    b, ti, tps = _seq_tile(TQ)
    nblk = TQ // ATTN_BLOCK

    def row_blocks(main_ref, prev_ref, next_ref, ctx_ref, g):
        blocks = [prev_ref[g]]
        blocks += [main_ref[g, j * ATTN_BLOCK:(j + 1) * ATTN_BLOCK, :] for j in range(nblk)]
        return blocks + [next_ref[g], ctx_ref[g]]

    ri = lax.broadcasted_iota(jnp.int32, (ATTN_BLOCK, ATTN_BLOCK), 0)
    ci = lax.broadcasted_iota(jnp.int32, (ATTN_BLOCK, ATTN_BLOCK), 1)
    lower_ok = ci >= ri
    upper_ok = ci <= ri
    first_ok = jnp.logical_and(lower_ok, ti > 0)
    last_ok = jnp.logical_and(upper_ok, ti < tps - 1)
    low = lax.broadcasted_iota(jnp.int32, (ATTN_BLOCK, LANE), 1) < HEAD_DIM

    mix_ref[:, :POOL_WIDTH] = yp_ref[...]
    for g in range(N_KV_HEADS):
        kblocks = ([kp_ref[g]] + [km_ref[g, :, j * ATTN_BLOCK:(j + 1) * ATTN_BLOCK] for j in range(nblk)]
                   + [kn_ref[g], kc_ref[g]])
        vblocks = row_blocks(vm_ref, vp_ref, vn_ref, vc_ref, g)
        for j in range(nblk):
            sel = [j, j + 1, j + 2, nblk + 2]
            rows = slice(j * ATTN_BLOCK, (j + 1) * ATTN_BLOCK)
            keys = jnp.concatenate([kblocks[s] for s in sel], axis=1)
            vals = jnp.concatenate([vblocks[s] for s in sel], axis=0)
            qg = jnp.concatenate([q_ref[GQA_GROUP * g + hh, rows, :] for hh in range(GQA_GROUP)], axis=0)
            s_all = jnp.dot(qg, keys, preferred_element_type=f32)
            m_prev = first_ok if j == 0 else lower_ok
            m_next = last_ok if j == nblk - 1 else upper_ok
            p_rows = []
            sink_terms = []
            for hh in range(GQA_GROUP):
                sink = sink_ref[GQA_GROUP * g + hh] * LOG2_E
                s = s_all[hh * ATTN_BLOCK:(hh + 1) * ATTN_BLOCK]
                s = jnp.concatenate([
                    jnp.where(m_prev, s[:, :ATTN_BLOCK], NEG_INF),
                    s[:, ATTN_BLOCK:2 * ATTN_BLOCK],
                    jnp.where(m_next, s[:, 2 * ATTN_BLOCK:3 * ATTN_BLOCK], NEG_INF),
                    s[:, 3 * ATTN_BLOCK:]], axis=1)
                m = jnp.maximum(jnp.max(s, axis=-1, keepdims=True), sink)
                p_rows.append(jnp.exp2(s - m).astype(bf16))
                sink_terms.append(jnp.exp2(sink - m))
            out = jnp.dot(jnp.concatenate(p_rows, axis=0), vals, preferred_element_type=f32)
            for pr in range(GQA_GROUP // 2):
                even = out[(2 * pr) * ATTN_BLOCK:(2 * pr + 1) * ATTN_BLOCK]
                odd = out[(2 * pr + 1) * ATTN_BLOCK:(2 * pr + 2) * ATTN_BLOCK]
                o = jnp.where(low, even / (pltpu.roll(even, HEAD_DIM, axis=1) + sink_terms[2 * pr]),
                              pltpu.roll(odd, HEAD_DIM, axis=1) / (odd + sink_terms[2 * pr + 1]))
                c0 = POOL_WIDTH + (2 * g + pr) * LANE
                mix_ref[rows, c0:c0 + LANE] = o.astype(bf16)

    y = jnp.dot(mix_ref[...], wout_ref[...], preferred_element_type=f32)
    o_ref[...] = x_ref[...] + _rms(y, _mod_row(mods_ref, GT_M, b) * g_ref[...])


def _even_attn(sink, q, k, v, kc, vc, yp, x2d, mods, g_post, w_out, later_weights):
    nsteps = N_TOK // TQ
    flat = [w.reshape(-1, w.shape[-1]) for w in later_weights]
    cast_specs = [pl.BlockSpec((w.shape[0] // nsteps, w.shape[1]), lambda i: (i, 0)) for w in flat]
    per = TQ // ATTN_BLOCK
    last = N_TOK // ATTN_BLOCK - 1
    tps = SEQ // TQ
    prev_blk = lambda i: jnp.maximum(i * per - 1, 0)
    next_blk = lambda i: jnp.minimum((i + 1) * per, last)
    k_band = [
        pl.BlockSpec((N_KV_HEADS, HEAD_DIM, TQ), lambda i: (0, 0, i)),
        pl.BlockSpec((N_KV_HEADS, HEAD_DIM, ATTN_BLOCK), lambda i: (0, 0, prev_blk(i))),
        pl.BlockSpec((N_KV_HEADS, HEAD_DIM, ATTN_BLOCK), lambda i: (0, 0, next_blk(i))),
    ]
    v_band = [
        pl.BlockSpec((N_KV_HEADS, TQ, LANE), lambda i: (0, i, 0)),
        pl.BlockSpec((N_KV_HEADS, ATTN_BLOCK, LANE), lambda i: (0, prev_blk(i), 0)),
        pl.BlockSpec((N_KV_HEADS, ATTN_BLOCK, LANE), lambda i: (0, next_blk(i), 0)),
    ]
    outs = pl.pallas_call(
        _even_attn_kernel,
        out_shape=[jax.ShapeDtypeStruct((N_TOK, D_MODEL), f32)]
                  + [jax.ShapeDtypeStruct(w.shape, bf16) for w in flat],
        grid=(nsteps,),
        in_specs=[pl.BlockSpec(memory_space=pltpu.SMEM),
                  pl.BlockSpec((N_Q_HEADS, TQ, HEAD_DIM), lambda i: (0, i, 0))]
                 + k_band + v_band + [
            pl.BlockSpec((N_KV_HEADS, HEAD_DIM, CTX_LEN), lambda i: (0, 0, i // tps)),
            pl.BlockSpec((N_KV_HEADS, CTX_LEN, LANE), lambda i: (0, i // tps, 0)),
            pl.BlockSpec((TQ, POOL_WIDTH), lambda i: (i, 0)),
            pl.BlockSpec((TQ, D_MODEL), lambda i: (i, 0)),
            _mods_spec(0),
            _layer_spec(g_post.shape, 0),
            _layer_spec(w_out.shape, 0, single_buffer=True),
        ] + cast_specs,
        out_specs=[pl.BlockSpec((TQ, D_MODEL), lambda i: (i, 0))] + cast_specs,
        scratch_shapes=[pltpu.VMEM((TQ, POOL_WIDTH + ATTN_WIDTH), bf16)],
        compiler_params=_params(1),
        name="even_attn",
    )(sink, q, k, k, k, v, v, v, kc, vc, yp, x2d, mods, g_post, w_out, *flat)
    return outs[0], [o.reshape(w.shape) for o, w in zip(outs[1:], later_weights)]


def _conv_ffn_kernel(xm_ref, xp_ref, xn_ref, mods_ref, gpre_ref, gpost_ref, wup_ref, cw_ref, cb_ref,
                     wdn_ref, o_ref, slab_ref, h_ref, act_ref):
    tm = TM_FFN
    nr = tm // SUBLANE
    nslab = D_MODEL // LANE
    b, ti, tps = _seq_tile(tm)
    sub = lax.broadcasted_iota(jnp.int32, (SUBLANE, FF_CHUNK), 0)

    def conv(hu, c0):
        w = cw_ref[:, c0:c0 + FF_CHUNK]
        first = jnp.where(sub == 0, pltpu.roll(hu[tm:tm + SUBLANE], 1, axis=0),
                          pltpu.roll(hu[tm - SUBLANE:tm], 1, axis=0))
        last = jnp.where(sub == SUBLANE - 1, pltpu.roll(hu[tm + SUBLANE:], SUBLANE - 1, axis=0),
                         pltpu.roll(hu[:SUBLANE], SUBLANE - 1, axis=0))
        before = jnp.concatenate([first, hu[:tm - SUBLANE]], axis=0)
        after = jnp.concatenate([hu[SUBLANE:tm], last], axis=0)
        return before * w[0:1] + hu[:tm] * w[1:2] + after * w[2:3] + cb_ref[:, c0:c0 + FF_CHUNK]

    g = gpre_ref[...]
    shift = _mod_row(mods_ref, SH_F, b)
    scale = _mod_row(mods_ref, SC_F, b)
    val = _pre(xm_ref[...], g, shift, scale)
    for j in range(nslab):
        for a in range(SUBLANE):
            slab_ref[j, a * SLAB_PITCH:a * SLAB_PITCH + nr, :] = val[a * nr:(a + 1) * nr, j * LANE:(j + 1) * LANE]
    for r in range(0, nr, 2):
        rows = [jnp.concatenate([slab_ref[j, pl.ds(r + k, SUBLANE, stride=SLAB_PITCH), :]
                                 for j in range(nslab)], axis=1) for k in range(2)]
        h_ref[r * SUBLANE:(r + 2) * SUBLANE, :] = jnp.concatenate(rows, axis=0).astype(bf16)
    hp = jnp.where(ti > 0, _pre(xp_ref[...], g, shift, scale), 0.0)
    hn = jnp.where(ti < tps - 1, _pre(xn_ref[...], g, shift, scale), 0.0)
    h_ref[tm:, :] = jnp.concatenate([hp, hn], axis=0).astype(bf16)
    h = h_ref[...]
    for c in range(D_FF // FF_CHUNK):
        g0 = c * FF_CHUNK
        u0 = D_FF + c * FF_CHUNK
        gate = conv(jnp.dot(h, wup_ref[:, g0:g0 + FF_CHUNK], preferred_element_type=f32), g0)
        up = conv(jnp.dot(h, wup_ref[:, u0:u0 + FF_CHUNK], preferred_element_type=f32), u0)
        act_ref[:, g0:g0 + FF_CHUNK] = (jax.nn.silu(gate) * up).astype(bf16)

    gain = _mod_row(mods_ref, GT_F, b) * gpost_ref[...]
    nrp = nr // FFN_DOWN_PARTS
    for part in range(FFN_DOWN_PARTS):
        p0 = part * nrp * SUBLANE
        res = _rms(jnp.dot(act_ref[p0:p0 + nrp * SUBLANE, :], wdn_ref[...], preferred_element_type=f32), gain)
        for r in range(nrp):
            for j in range(nslab):
                slab_ref[j, pl.ds(part * nrp + r, SUBLANE, stride=SLAB_PITCH), :] = (
                    res[r * SUBLANE:(r + 1) * SUBLANE, j * LANE:(j + 1) * LANE])
        for j in range(nslab):
            for a in range(SUBLANE):
                rows = slice(a * nr + part * nrp, a * nr + (part + 1) * nrp)
                seg = slice(a * SLAB_PITCH + part * nrp, a * SLAB_PITCH + (part + 1) * nrp)
                lanes = slice(j * LANE, (j + 1) * LANE)
                o_ref[rows, lanes] = xm_ref[rows, lanes] + slab_ref[j, seg, :]


def _conv_ffn(x2d, mods, layer, g_pre, g_post, w_up, conv_w, conv_b, w_down):
    tm = TM_FFN
    return pl.pallas_call(
        _conv_ffn_kernel,
        out_shape=jax.ShapeDtypeStruct((N_TOK, D_MODEL), f32),
        grid=(N_TOK // tm,),
        in_specs=_halo_specs(tm, D_MODEL, SUBLANE) + [
            _mods_spec(layer),
            _layer_spec(g_pre.shape, layer),
            _layer_spec(g_post.shape, layer),
            _layer_spec(w_up.shape, layer, single_buffer=True),
            _layer_spec(conv_w.shape, layer),
            _layer_spec(conv_b.shape, layer),
            _layer_spec(w_down.shape, layer, single_buffer=True),
        ],
        out_specs=pl.BlockSpec((tm, D_MODEL), lambda i: (i, 0)),
        scratch_shapes=[pltpu.VMEM((D_MODEL // LANE, SUBLANE * SLAB_PITCH, LANE), f32),
                        pltpu.VMEM((tm + 2 * SUBLANE, D_MODEL), bf16),
                        pltpu.VMEM((tm, D_FF), bf16)],
        compiler_params=_params(1),
        name="conv_ffn",
    )(x2d, x2d, x2d, mods, g_pre, g_post, w_up, conv_w, conv_b, w_down)


def _gelu_tanh(x):
    return x * jax.nn.sigmoid(x * (2.0 * GELU_C + (2.0 * GELU_C * 0.044715) * (x * x)))


def _odd_mixer_kernel(x_ref, mods_ref, gpre_ref, gpost_ref, win_ref, lng_ref, lnb_ref, ws_ref, bs_ref,
                      wout_ref, o_ref, gated_ref):
    tm = TM_ODD
    nch = tm // CHUNK
    gw = D_MODEL // N_SGU_GROUPS
    b, _, _ = _seq_tile(tm)
    x = x_ref[...]
    h = _pre(x, gpre_ref[...], _mod_row(mods_ref, SH_M, b), _mod_row(mods_ref, SC_M, b)).astype(bf16)
    def proj(lo):
        return jnp.concatenate(
            [_gelu_tanh(jnp.dot(h, win_ref[:, c0:c0 + ODD_CHUNK], preferred_element_type=f32))
             for c0 in range(lo, lo + D_MODEL, ODD_CHUNK)], axis=1)

    v = proj(D_MODEL)
    u = proj(0)
    mu = jnp.mean(v, axis=-1, keepdims=True)
    vc = v - mu
    var = jnp.mean(vc * vc, axis=-1, keepdims=True)
    vn = (vc * lax.rsqrt(var + EPS) * lng_ref[...] + lnb_ref[...]).astype(bf16)
    for g in range(N_SGU_GROUPS):
        lanes = slice(g * gw, (g + 1) * gw)
        vg = jnp.concatenate([vn[n * CHUNK:(n + 1) * CHUNK, lanes] for n in range(nch)], axis=1)
        s = jnp.dot(ws_ref[g], vg, preferred_element_type=f32)
        for n in range(nch):
            rows = slice(n * CHUNK, (n + 1) * CHUNK)
            gated_ref[rows, lanes] = (u[rows, lanes] * (s[:, n * gw:(n + 1) * gw] + bs_ref[g])).astype(bf16)
    y = jnp.dot(gated_ref[...], wout_ref[...], preferred_element_type=f32)
    o_ref[...] = x + _rms(y, _mod_row(mods_ref, GT_M, b) * gpost_ref[...])


def _odd_mixer(x2d, mods, g_pre, g_post, w_in, ln_g, ln_b, w_s, b_s, w_out):
    tm = TM_ODD
    return pl.pallas_call(
        _odd_mixer_kernel,
        out_shape=jax.ShapeDtypeStruct((N_TOK, D_MODEL), f32),
        grid=(N_TOK // tm,),
        in_specs=[
            pl.BlockSpec((tm, D_MODEL), lambda i: (i, 0)),
            _mods_spec(1),
            _layer_spec(g_pre.shape, 1),
            _layer_spec(g_post.shape, 1),
            _layer_spec(w_in.shape, 0, single_buffer=True),
            _layer_spec(ln_g.shape, 0),
            _layer_spec(ln_b.shape, 0),
            _layer_spec(w_s.shape, 0, single_buffer=True),
            pl.BlockSpec(b_s.shape, lambda i: (0, 0, 0), pipeline_mode=pl.Buffered(1)),
            _layer_spec(w_out.shape, 0, single_buffer=True),
        ],
        out_specs=pl.BlockSpec((tm, D_MODEL), lambda i: (i, 0)),
        scratch_shapes=[pltpu.VMEM((tm, D_MODEL), bf16)],
        compiler_params=_params(1),
        name="odd_mixer",
    )(x2d, mods, g_pre, g_post, w_in, ln_g, ln_b, w_s, b_s, w_out)


def _rope_tables():
    pos = np.arange(SEQ)
    inv = (np.float32(ROPE_BASE) ** (-np.arange(ROPE_FREQS, dtype=np.float32) / ROPE_FREQS)).astype(np.float32)
    ar = (pos // GRID_W).astype(np.float32)[:, None] * inv
    ac = (pos % GRID_W).astype(np.float32)[:, None] * inv
    cos = np.concatenate([np.cos(ar), np.cos(ar), np.cos(ac), np.cos(ac)] * 2, axis=1)
    sin = np.concatenate([-np.sin(ar), np.sin(ar), -np.sin(ac), np.sin(ac)] * 2, axis=1)
    q_scale = HEAD_DIM ** -0.5 * LOG2_E
    return np.stack([cos, sin, cos * q_scale, sin * q_scale]).astype(np.float32)


_ROPE_TABLES = _rope_tables()


def kernel(x, c, ctx, c_ctx, w_ada, b_ada, g_mix_pre, g_mix_post, g_ffn_pre, g_ffn_post, w_in_even, w_pool,
           pool_scale, attn_sink, w_out_even, w_in_odd, sgu_ln_g, sgu_ln_b, sgu_w, sgu_b, w_out_odd,
           w_ffn_up, ffn_conv_w, ffn_conv_b, w_ffn_down):
    assert x.shape == (BATCH, SEQ, D_MODEL) and ctx.shape == (BATCH, CTX_LEN, D_MODEL)
    x2d = x.reshape(N_TOK, D_MODEL)
    ctx2d = ctx.reshape(BATCH * CTX_LEN, D_MODEL)

    cvec = jnp.concatenate([c, c_ctx[None, :], jnp.zeros((8 - BATCH - 1, D_MODEL), f32)], axis=0)
    mods = _adaln(cvec, w_ada, b_ada)

    g_mix_pre, g_mix_post, g_ffn_pre, g_ffn_post = map(_stack_rows, (g_mix_pre, g_mix_post, g_ffn_pre, g_ffn_post))
    conv_b = _stack_rows(ffn_conv_b)

    w_in = w_in_even.astype(bf16)
    kc, vc = _ctx_kv(ctx2d, mods, g_mix_pre, w_in)
    yp, q, k, v = _even_in(x2d, mods, g_mix_pre, w_in, jnp.asarray(_ROPE_TABLES),
                           w_pool.astype(bf16), _stack_rows(pool_scale))
    x2d, (w_up, w_down, w_in_o, w_out_o, w_sgu) = _even_attn(
        attn_sink[0], q, k, v, kc, vc, yp, x2d, mods, g_mix_post, w_out_even.astype(bf16),
        [w_ffn_up, w_ffn_down, w_in_odd, w_out_odd, sgu_w])
    x2d = _conv_ffn(x2d, mods, 0, g_ffn_pre, g_ffn_post, w_up, ffn_conv_w, conv_b, w_down)

    b_s = jnp.broadcast_to(sgu_b[0][:, :, None], (N_SGU_GROUPS, CHUNK, CHUNK))
    x2d = _odd_mixer(x2d, mods, g_mix_pre, g_mix_post, w_in_o, _stack_rows(sgu_ln_g),
                     _stack_rows(sgu_ln_b), w_sgu, b_s, w_out_o)
    x2d = _conv_ffn(x2d, mods, 1, g_ffn_pre, g_ffn_post, w_up, ffn_conv_w, conv_b, w_down)
    return x2d.reshape(BATCH, SEQ, D_MODEL)
```

```python
import math

import numpy as np
import jax
import jax.numpy as jnp
from jax import lax
from jax.experimental import pallas as pl
from jax.experimental.pallas import tpu as pltpu

D_MODEL = 1024
BATCH = 4
SEQ = 4096
GRID_W = 64
CTX_LEN = 256
EPS = 1e-6
NEG_INF = -1e30
N_MOD = 6
POOL_WINDOWS = (2, 4, 8, 16)
POOL_GROUP_DIM = 128
POOL_WIDTH = 512
HEAD_DIM = 64
N_Q_HEADS = 8
N_KV_HEADS = 2
GQA_GROUP = N_Q_HEADS // N_KV_HEADS
ATTN_WIDTH = 512
KV_WIDTH = 128
ATTN_BLOCK = 128
ROPE_BASE = 10000.0
ROPE_FREQS = 16
Q_END = POOL_WIDTH + ATTN_WIDTH
CHUNK = 128
N_SGU_GROUPS = 8
D_FF = 2816
LOG2_E = math.log2(math.e)
GELU_C = math.sqrt(2.0 / math.pi)

LANE = 128
SUBLANE = 8
HALO = 16
VMEM_LIMIT = 56 * 1024 * 1024

N_TOK = BATCH * SEQ
TM_IN = 1024
TQ = 1024
TM_FFN = 1024
TM_ODD = 1024
FF_CHUNK = 256
FFN_DOWN_PARTS = 2
ADALN_SLOTS = 3
ODD_CHUNK = 512
SLAB_PITCH = TM_FFN // SUBLANE + SUBLANE

f32 = jnp.float32
bf16 = jnp.bfloat16

SH_M, SC_M, GT_M, SH_F, SC_F, GT_F = range(N_MOD)
CTX_ROW = BATCH


def _rms(x, g):
    return x * lax.rsqrt(jnp.mean(x * x, axis=-1, keepdims=True) + EPS) * g


def _pre(x, g, shift, scale):
    return _rms(x, g * (1.0 + scale)) + shift


def _mod_row(mods_ref, slot, row):
    return mods_ref[slot, pl.ds(row, 1), :]


def _layer_spec(shape, layer, single_buffer=False):
    zeros = (0,) * (len(shape) - 1)
    mode = dict(pipeline_mode=pl.Buffered(1)) if single_buffer else {}
    return pl.BlockSpec((None,) + tuple(shape[1:]), lambda *_: (layer,) + zeros, **mode)


def _mods_spec(layer):
    return pl.BlockSpec((None, N_MOD, 8, D_MODEL), lambda *_: (layer, 0, 0, 0))


def _params(n_axes=1):
    return pltpu.CompilerParams(dimension_semantics=("arbitrary",) * n_axes,
                                vmem_limit_bytes=VMEM_LIMIT)


def _stack_rows(a):
    return a.reshape(a.shape[0], 1, a.shape[1])


def _cast_riders(weights, nsteps, step_index):
    flat = [w.reshape(-1, w.shape[-1]) for w in weights]
    specs = [pl.BlockSpec((w.shape[0] // nsteps, w.shape[1]), lambda *i: (step_index(*i), 0)) for w in flat]
    return flat, specs, [jax.ShapeDtypeStruct(w.shape, bf16) for w in flat]


def _run_casts(cast_in, cast_out):
    for src, dst in zip(cast_in, cast_out):
        dst[...] = src[...].astype(bf16)


def _adaln_kernel(c_ref, w_ref, b_ref, *rest):
    n_cast = (len(rest) - 1) // 2
    o_ref = rest[n_cast]
    _run_casts(rest[:n_cast], rest[n_cast + 1:])
    s = jax.nn.silu(c_ref[...]).astype(bf16)
    z = jnp.dot(s, w_ref[...].astype(bf16), preferred_element_type=f32)
    for t in range(ADALN_SLOTS):
        o_ref[t] = z[:, t * D_MODEL:(t + 1) * D_MODEL] + b_ref[t]


def _adaln(cvec, w_ada, b_ada, early_weights):
    depth = w_ada.shape[0]
    per_layer = N_MOD // ADALN_SLOTS
    flat, cast_specs, cast_shapes = _cast_riders(early_weights, depth * per_layer, lambda l, j: l * per_layer + j)
    outs = pl.pallas_call(
        _adaln_kernel,
        out_shape=[jax.ShapeDtypeStruct((depth, N_MOD, 8, D_MODEL), f32)] + cast_shapes,
        grid=(depth, per_layer),
        in_specs=[
            pl.BlockSpec((8, D_MODEL), lambda l, j: (0, 0)),
            pl.BlockSpec((None, D_MODEL, ADALN_SLOTS * D_MODEL), lambda l, j: (l, 0, j)),
            pl.BlockSpec((None, ADALN_SLOTS, 1, D_MODEL), lambda l, j: (l, j, 0, 0)),
        ] + cast_specs,
        out_specs=[pl.BlockSpec((None, ADALN_SLOTS, 8, D_MODEL), lambda l, j: (l, j, 0, 0))] + cast_specs,
        compiler_params=_params(2),
        name="adaln",
    )(cvec, w_ada, b_ada.reshape(depth, N_MOD, 1, D_MODEL), *flat)
    return outs[0], [o.reshape(w.shape) for o, w in zip(outs[1:], early_weights)]


def _store_kv(k, v, k_ref, v_ref):
    low = lax.broadcasted_iota(jnp.int32, v.shape, 1) < HEAD_DIM
    swapped = pltpu.roll(v, HEAD_DIM, axis=1)
    kt = k.T.astype(bf16)
    for head, vals in enumerate((v, swapped)):
        k_ref[head] = kt[head * HEAD_DIM:(head + 1) * HEAD_DIM]
        v_ref[head] = jnp.where(low, vals, 1.0).astype(bf16)


def _ctx_kv_kernel(x_ref, mods_ref, g_ref, w_ref, k_ref, v_ref):
    h = _pre(x_ref[...], g_ref[...], mods_ref[SH_M, CTX_ROW:CTX_ROW + 1, :],
             mods_ref[SC_M, CTX_ROW:CTX_ROW + 1, :]).astype(bf16)
    z = jnp.dot(h, w_ref[:, Q_END:], preferred_element_type=f32)
    _store_kv(z[:, :KV_WIDTH], z[:, KV_WIDTH:], k_ref, v_ref)


def _ctx_kv(ctx2d, mods, g, w_in):
    n = ctx2d.shape[0]
    return pl.pallas_call(
        _ctx_kv_kernel,
        out_shape=(jax.ShapeDtypeStruct((N_KV_HEADS, HEAD_DIM, n), bf16),
                   jax.ShapeDtypeStruct((N_KV_HEADS, n, LANE), bf16)),
        grid=(1,),
        in_specs=[
            pl.BlockSpec((n, D_MODEL), lambda i: (0, 0)),
            _mods_spec(0),
            _layer_spec(g.shape, 0),
            _layer_spec(w_in.shape, 0),
        ],
        out_specs=(pl.BlockSpec((N_KV_HEADS, HEAD_DIM, n), lambda i: (0, 0, 0)),
                   pl.BlockSpec((N_KV_HEADS, n, LANE), lambda i: (0, 0, 0))),
        compiler_params=_params(1),
        name="ctx_kv",
    )(ctx2d, mods, g, w_in)


def _halo_specs(tm, width, halo=HALO):
    per = tm // halo
    last = N_TOK // halo - 1
    return [
        pl.BlockSpec((tm, width), lambda i: (i, 0)),
        pl.BlockSpec((halo, width), lambda i: (jnp.maximum(i * per - 1, 0), 0)),
        pl.BlockSpec((halo, width), lambda i: (jnp.minimum((i + 1) * per, last), 0)),
    ]


def _seq_tile(tm):
    tps = SEQ // tm
    i = pl.program_id(0)
    return i // tps, i % tps, tps


def _normed_with_halo(xm_ref, xp_ref, xn_ref, g, shift, scale, tm):
    _, ti, tps = _seq_tile(tm)
    hp = _pre(xp_ref[...], g, shift, scale)
    hn = _pre(xn_ref[...], g, shift, scale)
    hp = jnp.where(ti > 0, hp, 0.0)
    hn = jnp.where(ti < tps - 1, hn, 0.0)
    hm = _pre(xm_ref[...], g, shift, scale)
    return jnp.concatenate([hp, hm, hn], axis=0).astype(bf16)


def _rope(z, cos, sin_signed, lane):
    fwd = pltpu.roll(z, LANE - ROPE_FREQS, axis=1)
    bwd = pltpu.roll(z, ROPE_FREQS, axis=1)
    partner = jnp.where((lane % (2 * ROPE_FREQS)) < ROPE_FREQS, fwd, bwd)
    return z * cos + partner * sin_signed


def _even_in_kernel(xm_ref, xp_ref, xn_ref, mods_ref, g_ref, w_ref, rope_ref, wpool_ref, pscale_ref, *rest):
    n_cast = (len(rest) - 4) // 2
    yp_ref, q_ref, k_ref, v_ref = rest[n_cast:n_cast + 4]
    _run_casts(rest[:n_cast], rest[n_cast + 4:])
    tm = TM_IN
    b, ti, _ = _seq_tile(tm)
    shift = _mod_row(mods_ref, SH_M, b)
    scale = _mod_row(mods_ref, SC_M, b)
    h = _normed_with_halo(xm_ref, xp_ref, xn_ref, g_ref[...], shift, scale, tm)
    hm = h[HALO:HALO + tm]
    z = jnp.dot(h, w_ref[:, :POOL_WIDTH], preferred_element_type=f32)
    zq = jnp.dot(hm, w_ref[:, POOL_WIDTH:Q_END], preferred_element_type=f32)
    zkv = jnp.dot(hm, w_ref[:, Q_END:], preferred_element_type=f32)

    rows = tm + 2 * HALO
    edge = lax.broadcasted_iota(jnp.int32, (SUBLANE, POOL_GROUP_DIM), 0) + ti * tm
    for gi, win in enumerate(POOL_WINDOWS):
        lanes = slice(gi * POOL_GROUP_DIM, (gi + 1) * POOL_GROUP_DIM)
        u = z[:, lanes]
        s = u + pltpu.roll(u, 1, axis=0)
        step = 1
        while 2 * step < win:
            s = pltpu.roll(s, step, axis=0) + pltpu.roll(s, rows - step, axis=0)
            step *= 2
        half = win // 2
        count = lambda pos: (jnp.minimum(pos + half, SEQ) - jnp.maximum(pos - half, 0)).astype(f32)
        sm = s[HALO:HALO + tm]
        mean = jnp.concatenate([sm[:SUBLANE] / count(edge),
                                sm[SUBLANE:tm - SUBLANE] * (1.0 / win),
                                sm[tm - SUBLANE:] / count(edge + (tm - SUBLANE))], axis=0)
        pooled = (mean - u[HALO:HALO + tm]).astype(bf16)
        y = jnp.dot(pooled, wpool_ref[gi], preferred_element_type=f32)
        yp_ref[:, lanes] = (y * pscale_ref[:, lanes]).astype(bf16)

    lane = lax.broadcasted_iota(jnp.int32, (tm, LANE), 1)
    for j in range(ATTN_WIDTH // LANE):
        pair = _rope(zq[:, j * LANE:(j + 1) * LANE], rope_ref[2], rope_ref[3], lane)
        q_ref[2 * j] = pair[:, :HEAD_DIM].astype(bf16)
        q_ref[2 * j + 1] = pair[:, HEAD_DIM:].astype(bf16)
    _store_kv(_rope(zkv[:, :KV_WIDTH], rope_ref[0], rope_ref[1], lane), zkv[:, KV_WIDTH:], k_ref, v_ref)


def _even_in(x2d, mods, g, w_in, rope, w_pool, pool_scale, later_weights):
    tm = TM_IN
    tps = SEQ // tm
    heads = lambda n, w: pl.BlockSpec((n, tm, w), lambda i: (0, i, 0))
    flat, cast_specs, cast_shapes = _cast_riders(later_weights, N_TOK // tm, lambda i: i)
    outs = pl.pallas_call(
        _even_in_kernel,
        out_shape=[jax.ShapeDtypeStruct((N_TOK, POOL_WIDTH), bf16),
                   jax.ShapeDtypeStruct((N_Q_HEADS, N_TOK, HEAD_DIM), bf16),
                   jax.ShapeDtypeStruct((N_KV_HEADS, HEAD_DIM, N_TOK), bf16),
                   jax.ShapeDtypeStruct((N_KV_HEADS, N_TOK, LANE), bf16)] + cast_shapes,
        grid=(N_TOK // tm,),
        in_specs=_halo_specs(tm, D_MODEL) + [
            _mods_spec(0),
            _layer_spec(g.shape, 0),
            _layer_spec(w_in.shape, 0, single_buffer=True),
            pl.BlockSpec((rope.shape[0], tm, LANE), lambda i: (0, i % tps, 0)),
            _layer_spec(w_pool.shape, 0, single_buffer=True),
            _layer_spec(pool_scale.shape, 0),
        ] + cast_specs,
        out_specs=[pl.BlockSpec((tm, POOL_WIDTH), lambda i: (i, 0)), heads(N_Q_HEADS, HEAD_DIM),
                   pl.BlockSpec((N_KV_HEADS, HEAD_DIM, tm), lambda i: (0, 0, i)), heads(N_KV_HEADS, LANE)]
                  + cast_specs,
        compiler_params=_params(1),
        name="even_in",
    )(x2d, x2d, x2d, mods, g, w_in, rope, w_pool, pool_scale, *flat)
    return outs[:4], [o.reshape(w.shape) for o, w in zip(outs[4:], later_weights)]


def _even_attn_kernel(sink_ref, q_ref, km_ref, kp_ref, kn_ref, vm_ref, vp_ref, vn_ref,
                      kc_ref, vc_ref, yp_ref, x_ref, mods_ref, g_ref, wout_ref, *rest):
    n_cast = (len(rest) - 2) // 2
    o_ref, mix_ref = rest[n_cast], rest[-1]
    _run_casts(rest[:n_cast], rest[n_cast + 1:-1])
    b, ti, tps = _seq_tile(TQ)
    nblk = TQ // ATTN_BLOCK

    def row_blocks(main_ref, prev_ref, next_ref, ctx_ref, g):
        blocks = [prev_ref[g]]
        blocks += [main_ref[g, j * ATTN_BLOCK:(j + 1) * ATTN_BLOCK, :] for j in range(nblk)]
        return blocks + [next_ref[g], ctx_ref[g]]

    ri = lax.broadcasted_iota(jnp.int32, (ATTN_BLOCK, ATTN_BLOCK), 0)
    ci = lax.broadcasted_iota(jnp.int32, (ATTN_BLOCK, ATTN_BLOCK), 1)
    lower_ok = ci >= ri
    upper_ok = ci <= ri
    first_ok = jnp.logical_and(lower_ok, ti > 0)
    last_ok = jnp.logical_and(upper_ok, ti < tps - 1)
    low = lax.broadcasted_iota(jnp.int32, (ATTN_BLOCK, LANE), 1) < HEAD_DIM

    mix_ref[:, :POOL_WIDTH] = yp_ref[...]
    for g in range(N_KV_HEADS):
        kblocks = ([kp_ref[g]] + [km_ref[g, :, j * ATTN_BLOCK:(j + 1) * ATTN_BLOCK] for j in range(nblk)]
                   + [kn_ref[g], kc_ref[g]])
        vblocks = row_blocks(vm_ref, vp_ref, vn_ref, vc_ref, g)
        for j in range(nblk):
            sel = [j, j + 1, j + 2, nblk + 2]
            rows = slice(j * ATTN_BLOCK, (j + 1) * ATTN_BLOCK)
            keys = jnp.concatenate([kblocks[s] for s in sel], axis=1)
            vals = jnp.concatenate([vblocks[s] for s in sel], axis=0)
            qg = jnp.concatenate([q_ref[GQA_GROUP * g + hh, rows, :] for hh in range(GQA_GROUP)], axis=0)
            s_all = jnp.dot(qg, keys, preferred_element_type=f32)
            m_prev = first_ok if j == 0 else lower_ok
            m_next = last_ok if j == nblk - 1 else upper_ok
            p_rows = []
            sink_terms = []
            for hh in range(GQA_GROUP):
                sink = sink_ref[GQA_GROUP * g + hh] * LOG2_E
                s = s_all[hh * ATTN_BLOCK:(hh + 1) * ATTN_BLOCK]
                s = jnp.concatenate([
                    jnp.where(m_prev, s[:, :ATTN_BLOCK], NEG_INF),
                    s[:, ATTN_BLOCK:2 * ATTN_BLOCK],
                    jnp.where(m_next, s[:, 2 * ATTN_BLOCK:3 * ATTN_BLOCK], NEG_INF),
                    s[:, 3 * ATTN_BLOCK:]], axis=1)
                m = jnp.maximum(jnp.max(s, axis=-1, keepdims=True), sink)
                p_rows.append(jnp.exp2(s - m).astype(bf16))
                sink_terms.append(jnp.exp2(sink - m))
            out = jnp.dot(jnp.concatenate(p_rows, axis=0), vals, preferred_element_type=f32)
            for pr in range(GQA_GROUP // 2):
                even = out[(2 * pr) * ATTN_BLOCK:(2 * pr + 1) * ATTN_BLOCK]
                odd = out[(2 * pr + 1) * ATTN_BLOCK:(2 * pr + 2) * ATTN_BLOCK]
                o = jnp.where(low, even / (pltpu.roll(even, HEAD_DIM, axis=1) + sink_terms[2 * pr]),
                              pltpu.roll(odd, HEAD_DIM, axis=1) / (odd + sink_terms[2 * pr + 1]))
                c0 = POOL_WIDTH + (2 * g + pr) * LANE
                mix_ref[rows, c0:c0 + LANE] = o.astype(bf16)

    y = jnp.dot(mix_ref[...], wout_ref[...], preferred_element_type=f32)
    o_ref[...] = x_ref[...] + _rms(y, _mod_row(mods_ref, GT_M, b) * g_ref[...])


def _even_attn(sink, q, k, v, kc, vc, yp, x2d, mods, g_post, w_out, later_weights):
    nsteps = N_TOK // TQ
    flat, cast_specs, cast_shapes = _cast_riders(later_weights, nsteps, lambda i: i)
    per = TQ // ATTN_BLOCK
    last = N_TOK // ATTN_BLOCK - 1
    tps = SEQ // TQ
    prev_blk = lambda i: jnp.maximum(i * per - 1, 0)
    next_blk = lambda i: jnp.minimum((i + 1) * per, last)
    k_band = [
        pl.BlockSpec((N_KV_HEADS, HEAD_DIM, TQ), lambda i: (0, 0, i)),
        pl.BlockSpec((N_KV_HEADS, HEAD_DIM, ATTN_BLOCK), lambda i: (0, 0, prev_blk(i))),
        pl.BlockSpec((N_KV_HEADS, HEAD_DIM, ATTN_BLOCK), lambda i: (0, 0, next_blk(i))),
    ]
    v_band = [
        pl.BlockSpec((N_KV_HEADS, TQ, LANE), lambda i: (0, i, 0)),
        pl.BlockSpec((N_KV_HEADS, ATTN_BLOCK, LANE), lambda i: (0, prev_blk(i), 0)),
        pl.BlockSpec((N_KV_HEADS, ATTN_BLOCK, LANE), lambda i: (0, next_blk(i), 0)),
    ]
    outs = pl.pallas_call(
        _even_attn_kernel,
        out_shape=[jax.ShapeDtypeStruct((N_TOK, D_MODEL), f32)] + cast_shapes,
        grid=(nsteps,),
        in_specs=[pl.BlockSpec(memory_space=pltpu.SMEM),
                  pl.BlockSpec((N_Q_HEADS, TQ, HEAD_DIM), lambda i: (0, i, 0))]
                 + k_band + v_band + [
            pl.BlockSpec((N_KV_HEADS, HEAD_DIM, CTX_LEN), lambda i: (0, 0, i // tps)),
            pl.BlockSpec((N_KV_HEADS, CTX_LEN, LANE), lambda i: (0, i // tps, 0)),
            pl.BlockSpec((TQ, POOL_WIDTH), lambda i: (i, 0)),
            pl.BlockSpec((TQ, D_MODEL), lambda i: (i, 0)),
            _mods_spec(0),
            _layer_spec(g_post.shape, 0),
            _layer_spec(w_out.shape, 0, single_buffer=True),
        ] + cast_specs,
        out_specs=[pl.BlockSpec((TQ, D_MODEL), lambda i: (i, 0))] + cast_specs,
        scratch_shapes=[pltpu.VMEM((TQ, POOL_WIDTH + ATTN_WIDTH), bf16)],
        compiler_params=_params(1),
        name="even_attn",
    )(sink, q, k, k, k, v, v, v, kc, vc, yp, x2d, mods, g_post, w_out, *flat)
    return outs[0], [o.reshape(w.shape) for o, w in zip(outs[1:], later_weights)]


def _conv_ffn_kernel(xm_ref, xp_ref, xn_ref, mods_ref, gpre_ref, gpost_ref, wup_ref, cw_ref, cb_ref,
                     wdn_ref, o_ref, slab_ref, h_ref, act_ref):
    tm = TM_FFN
    nr = tm // SUBLANE
    nslab = D_MODEL // LANE
    b, ti, tps = _seq_tile(tm)
    sub = lax.broadcasted_iota(jnp.int32, (SUBLANE, FF_CHUNK), 0)

    def conv(hu, c0):
        w = cw_ref[:, c0:c0 + FF_CHUNK]
        first = jnp.where(sub == 0, pltpu.roll(hu[tm:tm + SUBLANE], 1, axis=0),
                          pltpu.roll(hu[tm - SUBLANE:tm], 1, axis=0))
        last = jnp.where(sub == SUBLANE - 1, pltpu.roll(hu[tm + SUBLANE:], SUBLANE - 1, axis=0),
                         pltpu.roll(hu[:SUBLANE], SUBLANE - 1, axis=0))
        before = jnp.concatenate([first, hu[:tm - SUBLANE]], axis=0)
        after = jnp.concatenate([hu[SUBLANE:tm], last], axis=0)
        return before * w[0:1] + hu[:tm] * w[1:2] + after * w[2:3] + cb_ref[:, c0:c0 + FF_CHUNK]

    g = gpre_ref[...]
    shift = _mod_row(mods_ref, SH_F, b)
    scale = _mod_row(mods_ref, SC_F, b)
    val = _pre(xm_ref[...], g, shift, scale)
    for j in range(nslab):
        for a in range(SUBLANE):
            slab_ref[j, a * SLAB_PITCH:a * SLAB_PITCH + nr, :] = val[a * nr:(a + 1) * nr, j * LANE:(j + 1) * LANE]
    for r in range(0, nr, 2):
        rows = [jnp.concatenate([slab_ref[j, pl.ds(r + k, SUBLANE, stride=SLAB_PITCH), :]
                                 for j in range(nslab)], axis=1) for k in range(2)]
        h_ref[r * SUBLANE:(r + 2) * SUBLANE, :] = jnp.concatenate(rows, axis=0).astype(bf16)
    hp = jnp.where(ti > 0, _pre(xp_ref[...], g, shift, scale), 0.0)
    hn = jnp.where(ti < tps - 1, _pre(xn_ref[...], g, shift, scale), 0.0)
    h_ref[tm:, :] = jnp.concatenate([hp, hn], axis=0).astype(bf16)
    h = h_ref[...]
    for c in range(D_FF // FF_CHUNK):
        g0 = c * FF_CHUNK
        u0 = D_FF + c * FF_CHUNK
        gate = conv(jnp.dot(h, wup_ref[:, g0:g0 + FF_CHUNK], preferred_element_type=f32), g0)
        up = conv(jnp.dot(h, wup_ref[:, u0:u0 + FF_CHUNK], preferred_element_type=f32), u0)
        act_ref[:, g0:g0 + FF_CHUNK] = (jax.nn.silu(gate) * up).astype(bf16)

    gain = _mod_row(mods_ref, GT_F, b) * gpost_ref[...]
    nrp = nr // FFN_DOWN_PARTS
    for part in range(FFN_DOWN_PARTS):
        p0 = part * nrp * SUBLANE
        res = _rms(jnp.dot(act_ref[p0:p0 + nrp * SUBLANE, :], wdn_ref[...], preferred_element_type=f32), gain)
        for r in range(nrp):
            for j in range(nslab):
                slab_ref[j, pl.ds(part * nrp + r, SUBLANE, stride=SLAB_PITCH), :] = (
                    res[r * SUBLANE:(r + 1) * SUBLANE, j * LANE:(j + 1) * LANE])
        for j in range(nslab):
            for a in range(SUBLANE):
                rows = slice(a * nr + part * nrp, a * nr + (part + 1) * nrp)
                seg = slice(a * SLAB_PITCH + part * nrp, a * SLAB_PITCH + (part + 1) * nrp)
                lanes = slice(j * LANE, (j + 1) * LANE)
                o_ref[rows, lanes] = xm_ref[rows, lanes] + slab_ref[j, seg, :]


def _conv_ffn(x2d, mods, layer, g_pre, g_post, w_up, conv_w, conv_b, w_down):
    tm = TM_FFN
    return pl.pallas_call(
        _conv_ffn_kernel,
        out_shape=jax.ShapeDtypeStruct((N_TOK, D_MODEL), f32),
        grid=(N_TOK // tm,),
        in_specs=_halo_specs(tm, D_MODEL, SUBLANE) + [
            _mods_spec(layer),
            _layer_spec(g_pre.shape, layer),
            _layer_spec(g_post.shape, layer),
            _layer_spec(w_up.shape, layer, single_buffer=True),
            _layer_spec(conv_w.shape, layer),
            _layer_spec(conv_b.shape, layer),
            _layer_spec(w_down.shape, layer, single_buffer=True),
        ],
        out_specs=pl.BlockSpec((tm, D_MODEL), lambda i: (i, 0)),
        scratch_shapes=[pltpu.VMEM((D_MODEL // LANE, SUBLANE * SLAB_PITCH, LANE), f32),
                        pltpu.VMEM((tm + 2 * SUBLANE, D_MODEL), bf16),
                        pltpu.VMEM((tm, D_FF), bf16)],
        compiler_params=_params(1),
        name="conv_ffn",
    )(x2d, x2d, x2d, mods, g_pre, g_post, w_up, conv_w, conv_b, w_down)


def _gelu_tanh(x):
    return x * jax.nn.sigmoid(x * (2.0 * GELU_C + (2.0 * GELU_C * 0.044715) * (x * x)))


def _odd_mixer_kernel(x_ref, mods_ref, gpre_ref, gpost_ref, win_ref, lng_ref, lnb_ref, ws_ref, bs_ref,
                      wout_ref, o_ref, gated_ref):
    tm = TM_ODD
    nch = tm // CHUNK
    gw = D_MODEL // N_SGU_GROUPS
    b, _, _ = _seq_tile(tm)
    x = x_ref[...]
    h = _pre(x, gpre_ref[...], _mod_row(mods_ref, SH_M, b), _mod_row(mods_ref, SC_M, b)).astype(bf16)
    def proj(lo):
        return jnp.concatenate(
            [_gelu_tanh(jnp.dot(h, win_ref[:, c0:c0 + ODD_CHUNK], preferred_element_type=f32))
             for c0 in range(lo, lo + D_MODEL, ODD_CHUNK)], axis=1)

    v = proj(D_MODEL)
    u = proj(0)
    mu = jnp.mean(v, axis=-1, keepdims=True)
    vc = v - mu
    var = jnp.mean(vc * vc, axis=-1, keepdims=True)
    vn = (vc * lax.rsqrt(var + EPS) * lng_ref[...] + lnb_ref[...]).astype(bf16)
    for g in range(N_SGU_GROUPS):
        lanes = slice(g * gw, (g + 1) * gw)
        vg = jnp.concatenate([vn[n * CHUNK:(n + 1) * CHUNK, lanes] for n in range(nch)], axis=1)
        s = jnp.dot(ws_ref[g], vg, preferred_element_type=f32)
        for n in range(nch):
            rows = slice(n * CHUNK, (n + 1) * CHUNK)
            gated_ref[rows, lanes] = (u[rows, lanes] * (s[:, n * gw:(n + 1) * gw] + bs_ref[g])).astype(bf16)
    y = jnp.dot(gated_ref[...], wout_ref[...], preferred_element_type=f32)
    o_ref[...] = x + _rms(y, _mod_row(mods_ref, GT_M, b) * gpost_ref[...])


def _odd_mixer(x2d, mods, g_pre, g_post, w_in, ln_g, ln_b, w_s, b_s, w_out):
    tm = TM_ODD
    return pl.pallas_call(
        _odd_mixer_kernel,
        out_shape=jax.ShapeDtypeStruct((N_TOK, D_MODEL), f32),
        grid=(N_TOK // tm,),
        in_specs=[
            pl.BlockSpec((tm, D_MODEL), lambda i: (i, 0)),
            _mods_spec(1),
            _layer_spec(g_pre.shape, 1),
            _layer_spec(g_post.shape, 1),
            _layer_spec(w_in.shape, 0, single_buffer=True),
            _layer_spec(ln_g.shape, 0),
            _layer_spec(ln_b.shape, 0),
            _layer_spec(w_s.shape, 0, single_buffer=True),
            pl.BlockSpec(b_s.shape, lambda i: (0, 0, 0), pipeline_mode=pl.Buffered(1)),
            _layer_spec(w_out.shape, 0, single_buffer=True),
        ],
        out_specs=pl.BlockSpec((tm, D_MODEL), lambda i: (i, 0)),
        scratch_shapes=[pltpu.VMEM((tm, D_MODEL), bf16)],
        compiler_params=_params(1),
        name="odd_mixer",
    )(x2d, mods, g_pre, g_post, w_in, ln_g, ln_b, w_s, b_s, w_out)


def _rope_tables():
    pos = np.arange(SEQ)
    inv = (np.float32(ROPE_BASE) ** (-np.arange(ROPE_FREQS, dtype=np.float32) / ROPE_FREQS)).astype(np.float32)
    ar = (pos // GRID_W).astype(np.float32)[:, None] * inv
    ac = (pos % GRID_W).astype(np.float32)[:, None] * inv
    cos = np.concatenate([np.cos(ar), np.cos(ar), np.cos(ac), np.cos(ac)] * 2, axis=1)
    sin = np.concatenate([-np.sin(ar), np.sin(ar), -np.sin(ac), np.sin(ac)] * 2, axis=1)
    q_scale = HEAD_DIM ** -0.5 * LOG2_E
    return np.stack([cos, sin, cos * q_scale, sin * q_scale]).astype(np.float32)


_ROPE_TABLES = _rope_tables()


def kernel(x, c, ctx, c_ctx, w_ada, b_ada, g_mix_pre, g_mix_post, g_ffn_pre, g_ffn_post, w_in_even, w_pool,
           pool_scale, attn_sink, w_out_even, w_in_odd, sgu_ln_g, sgu_ln_b, sgu_w, sgu_b, w_out_odd,
           w_ffn_up, ffn_conv_w, ffn_conv_b, w_ffn_down):
    assert x.shape == (BATCH, SEQ, D_MODEL) and ctx.shape == (BATCH, CTX_LEN, D_MODEL)
    x2d = x.reshape(N_TOK, D_MODEL)
    ctx2d = ctx.reshape(BATCH * CTX_LEN, D_MODEL)

    cvec = jnp.concatenate([c, c_ctx[None, :], jnp.zeros((8 - BATCH - 1, D_MODEL), f32)], axis=0)
    mods, (w_in, w_pool_b) = _adaln(cvec, w_ada, b_ada, [w_in_even, w_pool])

    g_mix_pre, g_mix_post, g_ffn_pre, g_ffn_post = map(_stack_rows, (g_mix_pre, g_mix_post, g_ffn_pre, g_ffn_post))
    conv_b = _stack_rows(ffn_conv_b)

    kc, vc = _ctx_kv(ctx2d, mods, g_mix_pre, w_in)
    (yp, q, k, v), (w_out_e,) = _even_in(x2d, mods, g_mix_pre, w_in, jnp.asarray(_ROPE_TABLES),
                                          w_pool_b, _stack_rows(pool_scale), [w_out_even])
    x2d, (w_up, w_down, w_in_o, w_out_o, w_sgu) = _even_attn(
        attn_sink[0], q, k, v, kc, vc, yp, x2d, mods, g_mix_post, w_out_e,
        [w_ffn_up, w_ffn_down, w_in_odd, w_out_odd, sgu_w])
    x2d = _conv_ffn(x2d, mods, 0, g_ffn_pre, g_ffn_post, w_up, ffn_conv_w, conv_b, w_down)

    b_s = jnp.broadcast_to(sgu_b[0][:, :, None], (N_SGU_GROUPS, CHUNK, CHUNK))
    x2d = _odd_mixer(x2d, mods, g_mix_pre, g_mix_post, w_in_o, _stack_rows(sgu_ln_g),
                     _stack_rows(sgu_ln_b), w_sgu, b_s, w_out_o)
    x2d = _conv_ffn(x2d, mods, 1, g_ffn_pre, g_ffn_post, w_up, ffn_conv_w, conv_b, w_down)
    return x2d.reshape(BATCH, SEQ, D_MODEL)
```

```python
import functools
import math

import numpy as np
import jax
import jax.numpy as jnp
from jax import lax
from jax.experimental import pallas as pl
from jax.experimental.pallas import tpu as pltpu

D_MODEL = 1024
BATCH = 4
SEQ = 4096
GRID_W = 64
CTX_LEN = 256
EPS = 1e-6
NEG_INF = -1e30
N_MOD = 6
POOL_WINDOWS = (2, 4, 8, 16)
POOL_GROUP_DIM = 128
POOL_WIDTH = 512
HEAD_DIM = 64
N_Q_HEADS = 8
N_KV_HEADS = 2
GQA_GROUP = N_Q_HEADS // N_KV_HEADS
ATTN_WIDTH = 512
KV_WIDTH = 128
ATTN_BLOCK = 128
ROPE_BASE = 10000.0
ROPE_FREQS = 16
Q_END = POOL_WIDTH + ATTN_WIDTH
CHUNK = 128
N_SGU_GROUPS = 8
D_FF = 2816
LOG2_E = math.log2(math.e)
GELU_C = math.sqrt(2.0 / math.pi)

LANE = 128
SUBLANE = 8
HALO = 16
VMEM_LIMIT = 56 * 1024 * 1024

N_TOK = BATCH * SEQ
TM_IN = 1024
TQ = 1024
TM_FFN = 1024
TM_ODD = 1024
FF_CHUNK = 256
FFN_DOWN_PARTS = 2
ADALN_SLOTS = 3
ODD_CHUNK = 512
SLAB_PITCH = TM_FFN // SUBLANE + SUBLANE

f32 = jnp.float32
bf16 = jnp.bfloat16

SH_M, SC_M, GT_M, SH_F, SC_F, GT_F = range(N_MOD)
CTX_ROW = BATCH


def _rms(x, g):
    return x * lax.rsqrt(jnp.mean(x * x, axis=-1, keepdims=True) + EPS) * g


def _pre(x, g, shift, scale):
    return _rms(x, g * (1.0 + scale)) + shift


def _mod_row(mods_ref, slot, row):
    return mods_ref[slot, pl.ds(row, 1), :]


def _layer_spec(shape, layer, single_buffer=False):
    zeros = (0,) * (len(shape) - 1)
    mode = dict(pipeline_mode=pl.Buffered(1)) if single_buffer else {}
    return pl.BlockSpec((None,) + tuple(shape[1:]), lambda *_: (layer,) + zeros, **mode)


def _mods_spec(layer):
    return pl.BlockSpec((None, N_MOD, 8, D_MODEL), lambda *_: (layer, 0, 0, 0))


def _params(n_axes=1):
    return pltpu.CompilerParams(dimension_semantics=("arbitrary",) * n_axes,
                                vmem_limit_bytes=VMEM_LIMIT)


def _table_spec(a):
    zeros = (0,) * a.ndim
    return pl.BlockSpec(a.shape, lambda *_: zeros)


def _cast_riders(weights, nsteps, step_index):
    flat = [w.reshape(-1, w.shape[-1]) for w in weights]
    specs = [pl.BlockSpec((w.shape[0] // nsteps, w.shape[1]), lambda *i: (step_index(*i), 0)) for w in flat]
    return flat, specs, [jax.ShapeDtypeStruct(w.shape, bf16) for w in flat]


def _run_casts(cast_in, cast_out):
    for src, dst in zip(cast_in, cast_out):
        dst[...] = src[...].astype(bf16)


def _adaln_kernel(c_ref, w_ref, b_ref, *rest):
    n_cast = (len(rest) - 1) // 2
    o_ref = rest[n_cast]
    _run_casts(rest[:n_cast], rest[n_cast + 1:])
    s = jax.nn.silu(c_ref[...]).astype(bf16)
    z = jnp.dot(s, w_ref[...].astype(bf16), preferred_element_type=f32)
    bias = b_ref[pl.ds(pl.program_id(0), 1), :]
    for t in range(ADALN_SLOTS):
        o_ref[t] = z[:, t * D_MODEL:(t + 1) * D_MODEL] + bias[:, t * D_MODEL:(t + 1) * D_MODEL]


def _adaln(cvec, w_ada, b_ada, early_weights):
    depth = w_ada.shape[0]
    per_layer = N_MOD // ADALN_SLOTS
    flat, cast_specs, cast_shapes = _cast_riders(early_weights, depth * per_layer, lambda l, j: l * per_layer + j)
    outs = pl.pallas_call(
        _adaln_kernel,
        out_shape=[jax.ShapeDtypeStruct((depth, N_MOD, 8, D_MODEL), f32)] + cast_shapes,
        grid=(depth, per_layer),
        in_specs=[
            pl.BlockSpec((8, D_MODEL), lambda l, j: (0, 0)),
            pl.BlockSpec((None, D_MODEL, ADALN_SLOTS * D_MODEL), lambda l, j: (l, 0, j)),
            pl.BlockSpec((depth, ADALN_SLOTS * D_MODEL), lambda l, j: (0, j)),
        ] + cast_specs,
        out_specs=[pl.BlockSpec((None, ADALN_SLOTS, 8, D_MODEL), lambda l, j: (l, j, 0, 0))] + cast_specs,
        compiler_params=_params(2),
        name="adaln",
    )(cvec, w_ada, b_ada, *flat)
    return outs[0], [o.reshape(w.shape) for o, w in zip(outs[1:], early_weights)]


def _store_kv(k, v, k_ref, v_ref):
    low = lax.broadcasted_iota(jnp.int32, v.shape, 1) < HEAD_DIM
    swapped = pltpu.roll(v, HEAD_DIM, axis=1)
    kt = k.T.astype(bf16)
    for head, vals in enumerate((v, swapped)):
        k_ref[head] = kt[head * HEAD_DIM:(head + 1) * HEAD_DIM]
        v_ref[head] = jnp.where(low, vals, 1.0).astype(bf16)


def _ctx_kv_kernel(x_ref, mods_ref, g_ref, w_ref, k_ref, v_ref):
    h = _pre(x_ref[...], g_ref[0:1, :], mods_ref[SH_M, CTX_ROW:CTX_ROW + 1, :],
             mods_ref[SC_M, CTX_ROW:CTX_ROW + 1, :]).astype(bf16)
    z = jnp.dot(h, w_ref[:, Q_END:], preferred_element_type=f32)
    _store_kv(z[:, :KV_WIDTH], z[:, KV_WIDTH:], k_ref, v_ref)


def _ctx_kv(ctx2d, mods, g, w_in):
    n = ctx2d.shape[0]
    return pl.pallas_call(
        _ctx_kv_kernel,
        out_shape=(jax.ShapeDtypeStruct((N_KV_HEADS, HEAD_DIM, n), bf16),
                   jax.ShapeDtypeStruct((N_KV_HEADS, n, LANE), bf16)),
        grid=(1,),
        in_specs=[
            pl.BlockSpec((n, D_MODEL), lambda i: (0, 0)),
            _mods_spec(0),
            _table_spec(g),
            _layer_spec(w_in.shape, 0),
        ],
        out_specs=(pl.BlockSpec((N_KV_HEADS, HEAD_DIM, n), lambda i: (0, 0, 0)),
                   pl.BlockSpec((N_KV_HEADS, n, LANE), lambda i: (0, 0, 0))),
        compiler_params=_params(1),
        name="ctx_kv",
    )(ctx2d, mods, g, w_in)


def _halo_specs(tm, width, halo=HALO):
    per = tm // halo
    last = N_TOK // halo - 1
    return [
        pl.BlockSpec((tm, width), lambda i: (i, 0)),
        pl.BlockSpec((halo, width), lambda i: (jnp.maximum(i * per - 1, 0), 0)),
        pl.BlockSpec((halo, width), lambda i: (jnp.minimum((i + 1) * per, last), 0)),
    ]


def _seq_tile(tm):
    tps = SEQ // tm
    i = pl.program_id(0)
    return i // tps, i % tps, tps


def _normed_with_halo(xm_ref, xp_ref, xn_ref, g, shift, scale, tm):
    _, ti, tps = _seq_tile(tm)
    hp = _pre(xp_ref[...], g, shift, scale)
    hn = _pre(xn_ref[...], g, shift, scale)
    hp = jnp.where(ti > 0, hp, 0.0)
    hn = jnp.where(ti < tps - 1, hn, 0.0)
    hm = _pre(xm_ref[...], g, shift, scale)
    return jnp.concatenate([hp, hm, hn], axis=0).astype(bf16)


def _rope(z, cos, sin_signed, lane):
    fwd = pltpu.roll(z, LANE - ROPE_FREQS, axis=1)
    bwd = pltpu.roll(z, ROPE_FREQS, axis=1)
    partner = jnp.where((lane % (2 * ROPE_FREQS)) < ROPE_FREQS, fwd, bwd)
    return z * cos + partner * sin_signed


def _even_in_kernel(xm_ref, xp_ref, xn_ref, mods_ref, g_ref, w_ref, rope_ref, wpool_ref, pscale_ref, *rest):
    n_cast = (len(rest) - 4) // 2
    yp_ref, q_ref, k_ref, v_ref = rest[n_cast:n_cast + 4]
    _run_casts(rest[:n_cast], rest[n_cast + 4:])
    tm = TM_IN
    b, ti, _ = _seq_tile(tm)
    shift = _mod_row(mods_ref, SH_M, b)
    scale = _mod_row(mods_ref, SC_M, b)
    h = _normed_with_halo(xm_ref, xp_ref, xn_ref, g_ref[0:1, :], shift, scale, tm)
    hm = h[HALO:HALO + tm]
    z = jnp.dot(h, w_ref[:, :POOL_WIDTH], preferred_element_type=f32)
    zq = jnp.dot(hm, w_ref[:, POOL_WIDTH:Q_END], preferred_element_type=f32)
    zkv = jnp.dot(hm, w_ref[:, Q_END:], preferred_element_type=f32)

    rows = tm + 2 * HALO
    edge = lax.broadcasted_iota(jnp.int32, (SUBLANE, POOL_GROUP_DIM), 0) + ti * tm
    for gi, win in enumerate(POOL_WINDOWS):
        lanes = slice(gi * POOL_GROUP_DIM, (gi + 1) * POOL_GROUP_DIM)
        u = z[:, lanes]
        s = u + pltpu.roll(u, 1, axis=0)
        step = 1
        while 2 * step < win:
            s = pltpu.roll(s, step, axis=0) + pltpu.roll(s, rows - step, axis=0)
            step *= 2
        half = win // 2
        count = lambda pos: (jnp.minimum(pos + half, SEQ) - jnp.maximum(pos - half, 0)).astype(f32)
        sm = s[HALO:HALO + tm]
        mean = jnp.concatenate([sm[:SUBLANE] / count(edge),
                                sm[SUBLANE:tm - SUBLANE] * (1.0 / win),
                                sm[tm - SUBLANE:] / count(edge + (tm - SUBLANE))], axis=0)
        pooled = (mean - u[HALO:HALO + tm]).astype(bf16)
        y = jnp.dot(pooled, wpool_ref[gi], preferred_element_type=f32)
        yp_ref[:, lanes] = (y * pscale_ref[:, lanes]).astype(bf16)

    lane = lax.broadcasted_iota(jnp.int32, (tm, LANE), 1)
    for j in range(ATTN_WIDTH // LANE):
        pair = _rope(zq[:, j * LANE:(j + 1) * LANE], rope_ref[2], rope_ref[3], lane)
        q_ref[2 * j] = pair[:, :HEAD_DIM].astype(bf16)
        q_ref[2 * j + 1] = pair[:, HEAD_DIM:].astype(bf16)
    _store_kv(_rope(zkv[:, :KV_WIDTH], rope_ref[0], rope_ref[1], lane), zkv[:, KV_WIDTH:], k_ref, v_ref)


def _even_in(x2d, mods, g, w_in, rope, w_pool, pool_scale, later_weights):
    tm = TM_IN
    tps = SEQ // tm
    heads = lambda n, w: pl.BlockSpec((n, tm, w), lambda i: (0, i, 0))
    flat, cast_specs, cast_shapes = _cast_riders(later_weights, N_TOK // tm, lambda i: i)
    outs = pl.pallas_call(
        _even_in_kernel,
        out_shape=[jax.ShapeDtypeStruct((N_TOK, POOL_WIDTH), bf16),
                   jax.ShapeDtypeStruct((N_Q_HEADS, N_TOK, HEAD_DIM), bf16),
                   jax.ShapeDtypeStruct((N_KV_HEADS, HEAD_DIM, N_TOK), bf16),
                   jax.ShapeDtypeStruct((N_KV_HEADS, N_TOK, LANE), bf16)] + cast_shapes,
        grid=(N_TOK // tm,),
        in_specs=_halo_specs(tm, D_MODEL) + [
            _mods_spec(0),
            _table_spec(g),
            _layer_spec(w_in.shape, 0, single_buffer=True),
            pl.BlockSpec((rope.shape[0], tm, LANE), lambda i: (0, i % tps, 0)),
            _layer_spec(w_pool.shape, 0, single_buffer=True),
            _table_spec(pool_scale),
        ] + cast_specs,
        out_specs=[pl.BlockSpec((tm, POOL_WIDTH), lambda i: (i, 0)), heads(N_Q_HEADS, HEAD_DIM),
                   pl.BlockSpec((N_KV_HEADS, HEAD_DIM, tm), lambda i: (0, 0, i)), heads(N_KV_HEADS, LANE)]
                  + cast_specs,
        compiler_params=_params(1),
        name="even_in",
    )(x2d, x2d, x2d, mods, g, w_in, rope, w_pool, pool_scale, *flat)
    return outs[:4], [o.reshape(w.shape) for o, w in zip(outs[4:], later_weights)]


def _even_attn_kernel(sink_ref, q_ref, km_ref, kp_ref, kn_ref, vm_ref, vp_ref, vn_ref,
                      kc_ref, vc_ref, yp_ref, x_ref, mods_ref, g_ref, wout_ref, *rest):
    n_cast = (len(rest) - 2) // 2
    o_ref, mix_ref = rest[n_cast], rest[-1]
    _run_casts(rest[:n_cast], rest[n_cast + 1:-1])
    b, ti, tps = _seq_tile(TQ)
    nblk = TQ // ATTN_BLOCK

    def row_blocks(main_ref, prev_ref, next_ref, ctx_ref, g):
        blocks = [prev_ref[g]]
        blocks += [main_ref[g, j * ATTN_BLOCK:(j + 1) * ATTN_BLOCK, :] for j in range(nblk)]
        return blocks + [next_ref[g], ctx_ref[g]]

    ri = lax.broadcasted_iota(jnp.int32, (ATTN_BLOCK, ATTN_BLOCK), 0)
    ci = lax.broadcasted_iota(jnp.int32, (ATTN_BLOCK, ATTN_BLOCK), 1)
    lower_ok = ci >= ri
    upper_ok = ci <= ri
    first_ok = jnp.logical_and(lower_ok, ti > 0)
    last_ok = jnp.logical_and(upper_ok, ti < tps - 1)
    low = lax.broadcasted_iota(jnp.int32, (ATTN_BLOCK, LANE), 1) < HEAD_DIM

    mix_ref[:, :POOL_WIDTH] = yp_ref[...]
    for g in range(N_KV_HEADS):
        kblocks = ([kp_ref[g]] + [km_ref[g, :, j * ATTN_BLOCK:(j + 1) * ATTN_BLOCK] for j in range(nblk)]
                   + [kn_ref[g], kc_ref[g]])
        vblocks = row_blocks(vm_ref, vp_ref, vn_ref, vc_ref, g)
        for j in range(nblk):
            sel = [j, j + 1, j + 2, nblk + 2]
            rows = slice(j * ATTN_BLOCK, (j + 1) * ATTN_BLOCK)
            keys = jnp.concatenate([kblocks[s] for s in sel], axis=1)
            vals = jnp.concatenate([vblocks[s] for s in sel], axis=0)
            qg = jnp.concatenate([q_ref[GQA_GROUP * g + hh, rows, :] for hh in range(GQA_GROUP)], axis=0)
            s_all = jnp.dot(qg, keys, preferred_element_type=f32)
            m_prev = first_ok if j == 0 else lower_ok
            m_next = last_ok if j == nblk - 1 else upper_ok
            p_rows = []
            sink_terms = []
            for hh in range(GQA_GROUP):
                sink = sink_ref[GQA_GROUP * g + hh] * LOG2_E
                s = s_all[hh * ATTN_BLOCK:(hh + 1) * ATTN_BLOCK]
                s = jnp.concatenate([
                    jnp.where(m_prev, s[:, :ATTN_BLOCK], NEG_INF),
                    s[:, ATTN_BLOCK:2 * ATTN_BLOCK],
                    jnp.where(m_next, s[:, 2 * ATTN_BLOCK:3 * ATTN_BLOCK], NEG_INF),
                    s[:, 3 * ATTN_BLOCK:]], axis=1)
                m = jnp.maximum(jnp.max(s, axis=-1, keepdims=True), sink)
                p_rows.append(jnp.exp2(s - m).astype(bf16))
                sink_terms.append(jnp.exp2(sink - m))
            out = jnp.dot(jnp.concatenate(p_rows, axis=0), vals, preferred_element_type=f32)
            for pr in range(GQA_GROUP // 2):
                even = out[(2 * pr) * ATTN_BLOCK:(2 * pr + 1) * ATTN_BLOCK]
                odd = out[(2 * pr + 1) * ATTN_BLOCK:(2 * pr + 2) * ATTN_BLOCK]
                o = jnp.where(low, even / (pltpu.roll(even, HEAD_DIM, axis=1) + sink_terms[2 * pr]),
                              pltpu.roll(odd, HEAD_DIM, axis=1) / (odd + sink_terms[2 * pr + 1]))
                c0 = POOL_WIDTH + (2 * g + pr) * LANE
                mix_ref[rows, c0:c0 + LANE] = o.astype(bf16)

    y = jnp.dot(mix_ref[...], wout_ref[...], preferred_element_type=f32)
    o_ref[...] = x_ref[...] + _rms(y, _mod_row(mods_ref, GT_M, b) * g_ref[0:1, :])


def _even_attn(sink, q, k, v, kc, vc, yp, x2d, mods, g_post, w_out, later_weights):
    nsteps = N_TOK // TQ
    flat, cast_specs, cast_shapes = _cast_riders(later_weights, nsteps, lambda i: i)
    per = TQ // ATTN_BLOCK
    last = N_TOK // ATTN_BLOCK - 1
    tps = SEQ // TQ
    prev_blk = lambda i: jnp.maximum(i * per - 1, 0)
    next_blk = lambda i: jnp.minimum((i + 1) * per, last)
    k_band = [
        pl.BlockSpec((N_KV_HEADS, HEAD_DIM, TQ), lambda i: (0, 0, i)),
        pl.BlockSpec((N_KV_HEADS, HEAD_DIM, ATTN_BLOCK), lambda i: (0, 0, prev_blk(i))),
        pl.BlockSpec((N_KV_HEADS, HEAD_DIM, ATTN_BLOCK), lambda i: (0, 0, next_blk(i))),
    ]
    v_band = [
        pl.BlockSpec((N_KV_HEADS, TQ, LANE), lambda i: (0, i, 0)),
        pl.BlockSpec((N_KV_HEADS, ATTN_BLOCK, LANE), lambda i: (0, prev_blk(i), 0)),
        pl.BlockSpec((N_KV_HEADS, ATTN_BLOCK, LANE), lambda i: (0, next_blk(i), 0)),
    ]
    outs = pl.pallas_call(
        _even_attn_kernel,
        out_shape=[jax.ShapeDtypeStruct((N_TOK, D_MODEL), f32)] + cast_shapes,
        grid=(nsteps,),
        in_specs=[pl.BlockSpec(memory_space=pltpu.SMEM),
                  pl.BlockSpec((N_Q_HEADS, TQ, HEAD_DIM), lambda i: (0, i, 0))]
                 + k_band + v_band + [
            pl.BlockSpec((N_KV_HEADS, HEAD_DIM, CTX_LEN), lambda i: (0, 0, i // tps)),
            pl.BlockSpec((N_KV_HEADS, CTX_LEN, LANE), lambda i: (0, i // tps, 0)),
            pl.BlockSpec((TQ, POOL_WIDTH), lambda i: (i, 0)),
            pl.BlockSpec((TQ, D_MODEL), lambda i: (i, 0)),
            _mods_spec(0),
            _table_spec(g_post),
            _layer_spec(w_out.shape, 0, single_buffer=True),
        ] + cast_specs,
        out_specs=[pl.BlockSpec((TQ, D_MODEL), lambda i: (i, 0))] + cast_specs,
        scratch_shapes=[pltpu.VMEM((TQ, POOL_WIDTH + ATTN_WIDTH), bf16)],
        compiler_params=_params(1),
        name="even_attn",
    )(sink, q, k, k, k, v, v, v, kc, vc, yp, x2d, mods, g_post, w_out, *flat)
    return outs[0], [o.reshape(w.shape) for o, w in zip(outs[1:], later_weights)]


def _conv_ffn_kernel(xm_ref, xp_ref, xn_ref, mods_ref, gpre_ref, gpost_ref, wup_ref, cw_ref, cb_ref,
                     wdn_ref, o_ref, slab_ref, h_ref, act_ref, *, layer):
    tm = TM_FFN
    nr = tm // SUBLANE
    nslab = D_MODEL // LANE
    b, ti, tps = _seq_tile(tm)
    sub = lax.broadcasted_iota(jnp.int32, (SUBLANE, FF_CHUNK), 0)

    def conv(hu, c0):
        w = cw_ref[layer, :, c0:c0 + FF_CHUNK]
        first = jnp.where(sub == 0, pltpu.roll(hu[tm:tm + SUBLANE], 1, axis=0),
                          pltpu.roll(hu[tm - SUBLANE:tm], 1, axis=0))
        last = jnp.where(sub == SUBLANE - 1, pltpu.roll(hu[tm + SUBLANE:], SUBLANE - 1, axis=0),
                         pltpu.roll(hu[:SUBLANE], SUBLANE - 1, axis=0))
        before = jnp.concatenate([first, hu[:tm - SUBLANE]], axis=0)
        after = jnp.concatenate([hu[SUBLANE:tm], last], axis=0)
        return (before * w[0:1] + hu[:tm] * w[1:2] + after * w[2:3]
                + cb_ref[layer:layer + 1, c0:c0 + FF_CHUNK])

    g = gpre_ref[layer:layer + 1, :]
    shift = _mod_row(mods_ref, SH_F, b)
    scale = _mod_row(mods_ref, SC_F, b)
    val = _pre(xm_ref[...], g, shift, scale)
    for j in range(nslab):
        for a in range(SUBLANE):
            slab_ref[j, a * SLAB_PITCH:a * SLAB_PITCH + nr, :] = val[a * nr:(a + 1) * nr, j * LANE:(j + 1) * LANE]
    for r in range(0, nr, 2):
        rows = [jnp.concatenate([slab_ref[j, pl.ds(r + k, SUBLANE, stride=SLAB_PITCH), :]
                                 for j in range(nslab)], axis=1) for k in range(2)]
        h_ref[r * SUBLANE:(r + 2) * SUBLANE, :] = jnp.concatenate(rows, axis=0).astype(bf16)
    hp = jnp.where(ti > 0, _pre(xp_ref[...], g, shift, scale), 0.0)
    hn = jnp.where(ti < tps - 1, _pre(xn_ref[...], g, shift, scale), 0.0)
    h_ref[tm:, :] = jnp.concatenate([hp, hn], axis=0).astype(bf16)
    h = h_ref[...]
    for c in range(D_FF // FF_CHUNK):
        g0 = c * FF_CHUNK
        u0 = D_FF + c * FF_CHUNK
        gate = conv(jnp.dot(h, wup_ref[:, g0:g0 + FF_CHUNK], preferred_element_type=f32), g0)
        up = conv(jnp.dot(h, wup_ref[:, u0:u0 + FF_CHUNK], preferred_element_type=f32), u0)
        act_ref[:, g0:g0 + FF_CHUNK] = (jax.nn.silu(gate) * up).astype(bf16)

    gain = _mod_row(mods_ref, GT_F, b) * gpost_ref[layer:layer + 1, :]
    nrp = nr // FFN_DOWN_PARTS
    for part in range(FFN_DOWN_PARTS):
        p0 = part * nrp * SUBLANE
        res = _rms(jnp.dot(act_ref[p0:p0 + nrp * SUBLANE, :], wdn_ref[...], preferred_element_type=f32), gain)
        for r in range(nrp):
            for j in range(nslab):
                slab_ref[j, pl.ds(part * nrp + r, SUBLANE, stride=SLAB_PITCH), :] = (
                    res[r * SUBLANE:(r + 1) * SUBLANE, j * LANE:(j + 1) * LANE])
        for j in range(nslab):
            for a in range(SUBLANE):
                rows = slice(a * nr + part * nrp, a * nr + (part + 1) * nrp)
                seg = slice(a * SLAB_PITCH + part * nrp, a * SLAB_PITCH + (part + 1) * nrp)
                lanes = slice(j * LANE, (j + 1) * LANE)
                o_ref[rows, lanes] = xm_ref[rows, lanes] + slab_ref[j, seg, :]


def _conv_ffn(x2d, mods, layer, g_pre, g_post, w_up, conv_w, conv_b, w_down):
    tm = TM_FFN
    return pl.pallas_call(
        functools.partial(_conv_ffn_kernel, layer=layer),
        out_shape=jax.ShapeDtypeStruct((N_TOK, D_MODEL), f32),
        grid=(N_TOK // tm,),
        in_specs=_halo_specs(tm, D_MODEL, SUBLANE) + [
            _mods_spec(layer),
            _table_spec(g_pre),
            _table_spec(g_post),
            _layer_spec(w_up.shape, layer, single_buffer=True),
            _table_spec(conv_w),
            _table_spec(conv_b),
            _layer_spec(w_down.shape, layer, single_buffer=True),
        ],
        out_specs=pl.BlockSpec((tm, D_MODEL), lambda i: (i, 0)),
        scratch_shapes=[pltpu.VMEM((D_MODEL // LANE, SUBLANE * SLAB_PITCH, LANE), f32),
                        pltpu.VMEM((tm + 2 * SUBLANE, D_MODEL), bf16),
                        pltpu.VMEM((tm, D_FF), bf16)],
        compiler_params=_params(1),
        name="conv_ffn",
    )(x2d, x2d, x2d, mods, g_pre, g_post, w_up, conv_w, conv_b, w_down)


def _gelu_tanh(x):
    return x * jax.nn.sigmoid(x * (2.0 * GELU_C + (2.0 * GELU_C * 0.044715) * (x * x)))


def _odd_mixer_kernel(x_ref, mods_ref, gpre_ref, gpost_ref, win_ref, lng_ref, lnb_ref, ws_ref, bs_ref,
                      wout_ref, o_ref, gated_ref):
    tm = TM_ODD
    nch = tm // CHUNK
    gw = D_MODEL // N_SGU_GROUPS
    b, _, _ = _seq_tile(tm)
    x = x_ref[...]
    h = _pre(x, gpre_ref[1:2, :], _mod_row(mods_ref, SH_M, b), _mod_row(mods_ref, SC_M, b)).astype(bf16)
    def proj(lo):
        return jnp.concatenate(
            [_gelu_tanh(jnp.dot(h, win_ref[:, c0:c0 + ODD_CHUNK], preferred_element_type=f32))
             for c0 in range(lo, lo + D_MODEL, ODD_CHUNK)], axis=1)

    v = proj(D_MODEL)
    u = proj(0)
    mu = jnp.mean(v, axis=-1, keepdims=True)
    vc = v - mu
    var = jnp.mean(vc * vc, axis=-1, keepdims=True)
    vn = (vc * lax.rsqrt(var + EPS) * lng_ref[...] + lnb_ref[...]).astype(bf16)
    for g in range(N_SGU_GROUPS):
        lanes = slice(g * gw, (g + 1) * gw)
        vg = jnp.concatenate([vn[n * CHUNK:(n + 1) * CHUNK, lanes] for n in range(nch)], axis=1)
        s = jnp.dot(ws_ref[g], vg, preferred_element_type=f32)
        for n in range(nch):
            rows = slice(n * CHUNK, (n + 1) * CHUNK)
            gated_ref[rows, lanes] = (u[rows, lanes] * (s[:, n * gw:(n + 1) * gw] + bs_ref[g])).astype(bf16)
    y = jnp.dot(gated_ref[...], wout_ref[...], preferred_element_type=f32)
    o_ref[...] = x + _rms(y, _mod_row(mods_ref, GT_M, b) * gpost_ref[1:2, :])


def _odd_mixer(x2d, mods, g_pre, g_post, w_in, ln_g, ln_b, w_s, b_s, w_out):
    tm = TM_ODD
    return pl.pallas_call(
        _odd_mixer_kernel,
        out_shape=jax.ShapeDtypeStruct((N_TOK, D_MODEL), f32),
        grid=(N_TOK // tm,),
        in_specs=[
            pl.BlockSpec((tm, D_MODEL), lambda i: (i, 0)),
            _mods_spec(1),
            _table_spec(g_pre),
            _table_spec(g_post),
            _layer_spec(w_in.shape, 0, single_buffer=True),
            _table_spec(ln_g),
            _table_spec(ln_b),
            _layer_spec(w_s.shape, 0, single_buffer=True),
            pl.BlockSpec(b_s.shape, lambda i: (0, 0, 0), pipeline_mode=pl.Buffered(1)),
            _layer_spec(w_out.shape, 0, single_buffer=True),
        ],
        out_specs=pl.BlockSpec((tm, D_MODEL), lambda i: (i, 0)),
        scratch_shapes=[pltpu.VMEM((tm, D_MODEL), bf16)],
        compiler_params=_params(1),
        name="odd_mixer",
    )(x2d, mods, g_pre, g_post, w_in, ln_g, ln_b, w_s, b_s, w_out)


def _rope_tables():
    pos = np.arange(SEQ)
    inv = (np.float32(ROPE_BASE) ** (-np.arange(ROPE_FREQS, dtype=np.float32) / ROPE_FREQS)).astype(np.float32)
    ar = (pos // GRID_W).astype(np.float32)[:, None] * inv
    ac = (pos % GRID_W).astype(np.float32)[:, None] * inv
    cos = np.concatenate([np.cos(ar), np.cos(ar), np.cos(ac), np.cos(ac)] * 2, axis=1)
    sin = np.concatenate([-np.sin(ar), np.sin(ar), -np.sin(ac), np.sin(ac)] * 2, axis=1)
    q_scale = HEAD_DIM ** -0.5 * LOG2_E
    return np.stack([cos, sin, cos * q_scale, sin * q_scale]).astype(np.float32)


_ROPE_TABLES = _rope_tables()


def kernel(x, c, ctx, c_ctx, w_ada, b_ada, g_mix_pre, g_mix_post, g_ffn_pre, g_ffn_post, w_in_even, w_pool,
           pool_scale, attn_sink, w_out_even, w_in_odd, sgu_ln_g, sgu_ln_b, sgu_w, sgu_b, w_out_odd,
           w_ffn_up, ffn_conv_w, ffn_conv_b, w_ffn_down):
    assert x.shape == (BATCH, SEQ, D_MODEL) and ctx.shape == (BATCH, CTX_LEN, D_MODEL)
    x2d = x.reshape(N_TOK, D_MODEL)
    ctx2d = ctx.reshape(BATCH * CTX_LEN, D_MODEL)

    cvec = jnp.concatenate([c, c_ctx[None, :], jnp.zeros((8 - BATCH - 1, D_MODEL), f32)], axis=0)
    mods, (w_in, w_pool_b) = _adaln(cvec, w_ada, b_ada, [w_in_even, w_pool])


    kc, vc = _ctx_kv(ctx2d, mods, g_mix_pre, w_in)
    (yp, q, k, v), (w_out_e,) = _even_in(x2d, mods, g_mix_pre, w_in, jnp.asarray(_ROPE_TABLES),
                                          w_pool_b, pool_scale, [w_out_even])
    x2d, (w_up, w_down, w_in_o, w_out_o, w_sgu) = _even_attn(
        attn_sink[0], q, k, v, kc, vc, yp, x2d, mods, g_mix_post, w_out_e,
        [w_ffn_up, w_ffn_down, w_in_odd, w_out_odd, sgu_w])
    x2d = _conv_ffn(x2d, mods, 0, g_ffn_pre, g_ffn_post, w_up, ffn_conv_w, ffn_conv_b, w_down)

    b_s = jnp.broadcast_to(sgu_b[0][:, :, None], (N_SGU_GROUPS, CHUNK, CHUNK))
    x2d = _odd_mixer(x2d, mods, g_mix_pre, g_mix_post, w_in_o, sgu_ln_g, sgu_ln_b, w_sgu, b_s, w_out_o)
    x2d = _conv_ffn(x2d, mods, 1, g_ffn_pre, g_ffn_post, w_up, ffn_conv_w, ffn_conv_b, w_down)
    return x2d.reshape(BATCH, SEQ, D_MODEL)
```

```python
import functools
import math

import numpy as np
import jax
import jax.numpy as jnp
from jax import lax
from jax.experimental import pallas as pl
from jax.experimental.pallas import tpu as pltpu

D_MODEL = 1024
BATCH = 4
SEQ = 4096
GRID_W = 64
CTX_LEN = 256
EPS = 1e-6
NEG_INF = -1e30
N_MOD = 6
POOL_WINDOWS = (2, 4, 8, 16)
POOL_GROUP_DIM = 128
POOL_WIDTH = 512
HEAD_DIM = 64
N_Q_HEADS = 8
N_KV_HEADS = 2
GQA_GROUP = N_Q_HEADS // N_KV_HEADS
ATTN_WIDTH = 512
KV_WIDTH = 128
ATTN_BLOCK = 128
ROPE_BASE = 10000.0
ROPE_FREQS = 16
Q_END = POOL_WIDTH + ATTN_WIDTH
CHUNK = 128
N_SGU_GROUPS = 8
D_FF = 2816
LOG2_E = math.log2(math.e)
GELU_C = math.sqrt(2.0 / math.pi)

LANE = 128
SUBLANE = 8
HALO = 16
VMEM_LIMIT = 56 * 1024 * 1024

N_TOK = BATCH * SEQ
TM_IN = 1024
TQ = 1024
TM_FFN = 1024
TM_ODD = 1024
FF_CHUNK = 256
FFN_DOWN_PARTS = 2
ADALN_SLOTS = 3
ODD_CHUNK = 512
SLAB_PITCH = TM_FFN // SUBLANE + SUBLANE

f32 = jnp.float32
bf16 = jnp.bfloat16

SH_M, SC_M, GT_M, SH_F, SC_F, GT_F = range(N_MOD)
CTX_ROW = BATCH


def _rms(x, g):
    return x * lax.rsqrt(jnp.mean(x * x, axis=-1, keepdims=True) + EPS) * g


def _pre(x, g, shift, scale):
    return _rms(x, g * (1.0 + scale)) + shift


def _silu(x):
    return x / (1.0 + jnp.exp2(x * -LOG2_E))


def _mod_row(mods_ref, slot, row):
    return mods_ref[slot, pl.ds(row, 1), :]


def _layer_spec(shape, layer, single_buffer=False):
    zeros = (0,) * (len(shape) - 1)
    mode = dict(pipeline_mode=pl.Buffered(1)) if single_buffer else {}
    return pl.BlockSpec((None,) + tuple(shape[1:]), lambda *_: (layer,) + zeros, **mode)


def _mods_spec(layer):
    return pl.BlockSpec((None, N_MOD, SUBLANE, D_MODEL), lambda *_: (layer, 0, 0, 0))


def _params(n_axes=1):
    return pltpu.CompilerParams(dimension_semantics=("arbitrary",) * n_axes,
                                vmem_limit_bytes=VMEM_LIMIT)


def _table_spec(a):
    zeros = (0,) * a.ndim
    return pl.BlockSpec(a.shape, lambda *_: zeros)


def _cast_riders(weights, nsteps, step_index):
    flat = [w.reshape(-1, w.shape[-1]) for w in weights]
    specs = [pl.BlockSpec((w.shape[0] // nsteps, w.shape[1]), lambda *i: (step_index(*i), 0)) for w in flat]
    return flat, specs, [jax.ShapeDtypeStruct(w.shape, bf16) for w in flat]


def _run_casts(cast_in, cast_out):
    for src, dst in zip(cast_in, cast_out):
        dst[...] = src[...].astype(bf16)


def _adaln_kernel(c_ref, w_ref, b_ref, *rest):
    n_cast = (len(rest) - 1) // 2
    o_ref = rest[n_cast]
    _run_casts(rest[:n_cast], rest[n_cast + 1:])
    s = jax.nn.silu(c_ref[...]).astype(bf16)
    z = jnp.dot(s, w_ref[...].astype(bf16), preferred_element_type=f32)
    bias = b_ref[pl.ds(pl.program_id(0), 1), :]
    for t in range(ADALN_SLOTS):
        o_ref[t] = z[:, t * D_MODEL:(t + 1) * D_MODEL] + bias[:, t * D_MODEL:(t + 1) * D_MODEL]


def _adaln(cvec, w_ada, b_ada, early_weights):
    depth = w_ada.shape[0]
    per_layer = N_MOD // ADALN_SLOTS
    flat, cast_specs, cast_shapes = _cast_riders(early_weights, depth * per_layer, lambda l, j: l * per_layer + j)
    outs = pl.pallas_call(
        _adaln_kernel,
        out_shape=[jax.ShapeDtypeStruct((depth, N_MOD, SUBLANE, D_MODEL), f32)] + cast_shapes,
        grid=(depth, per_layer),
        in_specs=[
            pl.BlockSpec((SUBLANE, D_MODEL), lambda l, j: (0, 0)),
            pl.BlockSpec((None, D_MODEL, ADALN_SLOTS * D_MODEL), lambda l, j: (l, 0, j)),
            pl.BlockSpec((depth, ADALN_SLOTS * D_MODEL), lambda l, j: (0, j)),
        ] + cast_specs,
        out_specs=[pl.BlockSpec((None, ADALN_SLOTS, SUBLANE, D_MODEL), lambda l, j: (l, j, 0, 0))] + cast_specs,
        compiler_params=_params(2),
        name="adaln",
    )(cvec, w_ada, b_ada, *flat)
    return outs[0], [o.reshape(w.shape) for o, w in zip(outs[1:], early_weights)]


def _store_kv(k, v, k_ref, v_ref):
    low = lax.broadcasted_iota(jnp.int32, v.shape, 1) < HEAD_DIM
    swapped = pltpu.roll(v, HEAD_DIM, axis=1)
    kt = k.T.astype(bf16)
    for head, vals in enumerate((v, swapped)):
        k_ref[head] = kt[head * HEAD_DIM:(head + 1) * HEAD_DIM]
        v_ref[head] = jnp.where(low, vals, 1.0).astype(bf16)


def _ctx_kv_kernel(x_ref, mods_ref, g_ref, w_ref, k_ref, v_ref):
    h = _pre(x_ref[...], g_ref[0:1, :], mods_ref[SH_M, CTX_ROW:CTX_ROW + 1, :],
             mods_ref[SC_M, CTX_ROW:CTX_ROW + 1, :]).astype(bf16)
    z = jnp.dot(h, w_ref[:, Q_END:], preferred_element_type=f32)
    _store_kv(z[:, :KV_WIDTH], z[:, KV_WIDTH:], k_ref, v_ref)


def _ctx_kv(ctx2d, mods, g, w_in):
    n = ctx2d.shape[0]
    return pl.pallas_call(
        _ctx_kv_kernel,
        out_shape=(jax.ShapeDtypeStruct((N_KV_HEADS, HEAD_DIM, n), bf16),
                   jax.ShapeDtypeStruct((N_KV_HEADS, n, LANE), bf16)),
        grid=(1,),
        in_specs=[
            pl.BlockSpec((n, D_MODEL), lambda i: (0, 0)),
            _mods_spec(0),
            _table_spec(g),
            _layer_spec(w_in.shape, 0),
        ],
        out_specs=(pl.BlockSpec((N_KV_HEADS, HEAD_DIM, n), lambda i: (0, 0, 0)),
                   pl.BlockSpec((N_KV_HEADS, n, LANE), lambda i: (0, 0, 0))),
        compiler_params=_params(1),
        name="ctx_kv",
    )(ctx2d, mods, g, w_in)


def _halo_specs(tm, width, halo=HALO):
    per = tm // halo
    last = N_TOK // halo - 1
    return [
        pl.BlockSpec((tm, width), lambda i: (i, 0)),
        pl.BlockSpec((halo, width), lambda i: (jnp.maximum(i * per - 1, 0), 0)),
        pl.BlockSpec((halo, width), lambda i: (jnp.minimum((i + 1) * per, last), 0)),
    ]


def _seq_tile(tm):
    tps = SEQ // tm
    i = pl.program_id(0)
    return i // tps, i % tps, tps


def _normed_with_halo(xm_ref, xp_ref, xn_ref, g, shift, scale, tm):
    _, ti, tps = _seq_tile(tm)
    hp = _pre(xp_ref[...], g, shift, scale)
    hn = _pre(xn_ref[...], g, shift, scale)
    hp = jnp.where(ti > 0, hp, 0.0)
    hn = jnp.where(ti < tps - 1, hn, 0.0)
    hm = _pre(xm_ref[...], g, shift, scale)
    return jnp.concatenate([hp, hm, hn], axis=0).astype(bf16)


def _rope(z, cos, sin_signed, lane):
    fwd = pltpu.roll(z, LANE - ROPE_FREQS, axis=1)
    bwd = pltpu.roll(z, ROPE_FREQS, axis=1)
    partner = jnp.where((lane % (2 * ROPE_FREQS)) < ROPE_FREQS, fwd, bwd)
    return z * cos + partner * sin_signed


def _even_in_kernel(xm_ref, xp_ref, xn_ref, mods_ref, g_ref, w_ref, rope_ref, wpool_ref, pscale_ref, *rest):
    n_cast = (len(rest) - 4) // 2
    yp_ref, q_ref, k_ref, v_ref = rest[n_cast:n_cast + 4]
    _run_casts(rest[:n_cast], rest[n_cast + 4:])
    tm = TM_IN
    b, ti, _ = _seq_tile(tm)
    shift = _mod_row(mods_ref, SH_M, b)
    scale = _mod_row(mods_ref, SC_M, b)
    h = _normed_with_halo(xm_ref, xp_ref, xn_ref, g_ref[0:1, :], shift, scale, tm)
    hm = h[HALO:HALO + tm]
    z = jnp.dot(h, w_ref[:, :POOL_WIDTH], preferred_element_type=f32)
    zq = jnp.dot(hm, w_ref[:, POOL_WIDTH:Q_END], preferred_element_type=f32)
    zkv = jnp.dot(hm, w_ref[:, Q_END:], preferred_element_type=f32)

    rows = tm + 2 * HALO
    edge = lax.broadcasted_iota(jnp.int32, (SUBLANE, POOL_GROUP_DIM), 0) + ti * tm
    for gi, win in enumerate(POOL_WINDOWS):
        lanes = slice(gi * POOL_GROUP_DIM, (gi + 1) * POOL_GROUP_DIM)
        u = z[:, lanes]
        s = u + pltpu.roll(u, 1, axis=0)
        step = 1
        while 2 * step < win:
            s = pltpu.roll(s, step, axis=0) + pltpu.roll(s, rows - step, axis=0)
            step *= 2
        half = win // 2
        count = lambda pos: (jnp.minimum(pos + half, SEQ) - jnp.maximum(pos - half, 0)).astype(f32)
        sm = s[HALO:HALO + tm]
        mean = jnp.concatenate([sm[:SUBLANE] / count(edge),
                                sm[SUBLANE:tm - SUBLANE] * (1.0 / win),
                                sm[tm - SUBLANE:] / count(edge + (tm - SUBLANE))], axis=0)
        pooled = (mean - u[HALO:HALO + tm]).astype(bf16)
        y = jnp.dot(pooled, wpool_ref[gi], preferred_element_type=f32)
        yp_ref[:, lanes] = (y * pscale_ref[:, lanes]).astype(bf16)

    lane = lax.broadcasted_iota(jnp.int32, (tm, LANE), 1)
    for j in range(ATTN_WIDTH // LANE):
        pair = _rope(zq[:, j * LANE:(j + 1) * LANE], rope_ref[2], rope_ref[3], lane)
        q_ref[2 * j] = pair[:, :HEAD_DIM].astype(bf16)
        q_ref[2 * j + 1] = pair[:, HEAD_DIM:].astype(bf16)
    _store_kv(_rope(zkv[:, :KV_WIDTH], rope_ref[0], rope_ref[1], lane), zkv[:, KV_WIDTH:], k_ref, v_ref)


def _even_in(x2d, mods, g, w_in, rope, w_pool, pool_scale, later_weights):
    tm = TM_IN
    tps = SEQ // tm
    heads = lambda n, w: pl.BlockSpec((n, tm, w), lambda i: (0, i, 0))
    flat, cast_specs, cast_shapes = _cast_riders(later_weights, N_TOK // tm, lambda i: i)
    outs = pl.pallas_call(
        _even_in_kernel,
        out_shape=[jax.ShapeDtypeStruct((N_TOK, POOL_WIDTH), bf16),
                   jax.ShapeDtypeStruct((N_Q_HEADS, N_TOK, HEAD_DIM), bf16),
                   jax.ShapeDtypeStruct((N_KV_HEADS, HEAD_DIM, N_TOK), bf16),
                   jax.ShapeDtypeStruct((N_KV_HEADS, N_TOK, LANE), bf16)] + cast_shapes,
        grid=(N_TOK // tm,),
        in_specs=_halo_specs(tm, D_MODEL) + [
            _mods_spec(0),
            _table_spec(g),
            _layer_spec(w_in.shape, 0, single_buffer=True),
            pl.BlockSpec((rope.shape[0], tm, LANE), lambda i: (0, i % tps, 0)),
            _layer_spec(w_pool.shape, 0, single_buffer=True),
            _table_spec(pool_scale),
        ] + cast_specs,
        out_specs=[pl.BlockSpec((tm, POOL_WIDTH), lambda i: (i, 0)), heads(N_Q_HEADS, HEAD_DIM),
                   pl.BlockSpec((N_KV_HEADS, HEAD_DIM, tm), lambda i: (0, 0, i)), heads(N_KV_HEADS, LANE)]
                  + cast_specs,
        compiler_params=_params(1),
        name="even_in",
    )(x2d, x2d, x2d, mods, g, w_in, rope, w_pool, pool_scale, *flat)
    return outs[:4], [o.reshape(w.shape) for o, w in zip(outs[4:], later_weights)]


def _even_attn_kernel(sink_ref, q_ref, km_ref, kp_ref, kn_ref, vm_ref, vp_ref, vn_ref,
                      kc_ref, vc_ref, yp_ref, x_ref, mods_ref, g_ref, wout_ref, *rest):
    n_cast = (len(rest) - 2) // 2
    o_ref, mix_ref = rest[n_cast], rest[-1]
    _run_casts(rest[:n_cast], rest[n_cast + 1:-1])
    b, ti, tps = _seq_tile(TQ)
    nblk = TQ // ATTN_BLOCK

    def row_blocks(main_ref, prev_ref, next_ref, ctx_ref, g):
        blocks = [prev_ref[g]]
        blocks += [main_ref[g, j * ATTN_BLOCK:(j + 1) * ATTN_BLOCK, :] for j in range(nblk)]
        return blocks + [next_ref[g], ctx_ref[g]]

    ri = lax.broadcasted_iota(jnp.int32, (ATTN_BLOCK, ATTN_BLOCK), 0)
    ci = lax.broadcasted_iota(jnp.int32, (ATTN_BLOCK, ATTN_BLOCK), 1)
    lower_ok = ci >= ri
    upper_ok = ci <= ri
    first_ok = jnp.logical_and(lower_ok, ti > 0)
    last_ok = jnp.logical_and(upper_ok, ti < tps - 1)
    low = lax.broadcasted_iota(jnp.int32, (ATTN_BLOCK, LANE), 1) < HEAD_DIM

    mix_ref[:, :POOL_WIDTH] = yp_ref[...]
    for g in range(N_KV_HEADS):
        kblocks = ([kp_ref[g]] + [km_ref[g, :, j * ATTN_BLOCK:(j + 1) * ATTN_BLOCK] for j in range(nblk)]
                   + [kn_ref[g], kc_ref[g]])
        vblocks = row_blocks(vm_ref, vp_ref, vn_ref, vc_ref, g)
        for j in range(nblk):
            sel = [j, j + 1, j + 2, nblk + 2]
            rows = slice(j * ATTN_BLOCK, (j + 1) * ATTN_BLOCK)
            keys = jnp.concatenate([kblocks[s] for s in sel], axis=1)
            vals = jnp.concatenate([vblocks[s] for s in sel], axis=0)
            qg = jnp.concatenate([q_ref[GQA_GROUP * g + hh, rows, :] for hh in range(GQA_GROUP)], axis=0)
            s_all = jnp.dot(qg, keys, preferred_element_type=f32)
            m_prev = first_ok if j == 0 else lower_ok
            m_next = last_ok if j == nblk - 1 else upper_ok
            p_rows = []
            sink_terms = []
            for hh in range(GQA_GROUP):
                sink = sink_ref[GQA_GROUP * g + hh] * LOG2_E
                s = s_all[hh * ATTN_BLOCK:(hh + 1) * ATTN_BLOCK]
                s = jnp.concatenate([
                    jnp.where(m_prev, s[:, :ATTN_BLOCK], NEG_INF),
                    s[:, ATTN_BLOCK:2 * ATTN_BLOCK],
                    jnp.where(m_next, s[:, 2 * ATTN_BLOCK:3 * ATTN_BLOCK], NEG_INF),
                    s[:, 3 * ATTN_BLOCK:]], axis=1)
                m = jnp.maximum(jnp.max(s, axis=-1, keepdims=True), sink)
                p_rows.append(jnp.exp2(s - m).astype(bf16))
                sink_terms.append(jnp.exp2(sink - m))
            out = jnp.dot(jnp.concatenate(p_rows, axis=0), vals, preferred_element_type=f32)
            for pr in range(GQA_GROUP // 2):
                even = out[(2 * pr) * ATTN_BLOCK:(2 * pr + 1) * ATTN_BLOCK]
                odd = out[(2 * pr + 1) * ATTN_BLOCK:(2 * pr + 2) * ATTN_BLOCK]
                o = jnp.where(low, even / (pltpu.roll(even, HEAD_DIM, axis=1) + sink_terms[2 * pr]),
                              pltpu.roll(odd, HEAD_DIM, axis=1) / (odd + sink_terms[2 * pr + 1]))
                c0 = POOL_WIDTH + (2 * g + pr) * LANE
                mix_ref[rows, c0:c0 + LANE] = o.astype(bf16)

    y = jnp.dot(mix_ref[...], wout_ref[...], preferred_element_type=f32)
    o_ref[...] = x_ref[...] + _rms(y, _mod_row(mods_ref, GT_M, b) * g_ref[0:1, :])


def _even_attn(sink, q, k, v, kc, vc, yp, x2d, mods, g_post, w_out, later_weights):
    nsteps = N_TOK // TQ
    flat, cast_specs, cast_shapes = _cast_riders(later_weights, nsteps, lambda i: i)
    per = TQ // ATTN_BLOCK
    last = N_TOK // ATTN_BLOCK - 1
    tps = SEQ // TQ
    prev_blk = lambda i: jnp.maximum(i * per - 1, 0)
    next_blk = lambda i: jnp.minimum((i + 1) * per, last)
    k_band = [
        pl.BlockSpec((N_KV_HEADS, HEAD_DIM, TQ), lambda i: (0, 0, i)),
        pl.BlockSpec((N_KV_HEADS, HEAD_DIM, ATTN_BLOCK), lambda i: (0, 0, prev_blk(i))),
        pl.BlockSpec((N_KV_HEADS, HEAD_DIM, ATTN_BLOCK), lambda i: (0, 0, next_blk(i))),
    ]
    v_band = [
        pl.BlockSpec((N_KV_HEADS, TQ, LANE), lambda i: (0, i, 0)),
        pl.BlockSpec((N_KV_HEADS, ATTN_BLOCK, LANE), lambda i: (0, prev_blk(i), 0)),
        pl.BlockSpec((N_KV_HEADS, ATTN_BLOCK, LANE), lambda i: (0, next_blk(i), 0)),
    ]
    outs = pl.pallas_call(
        _even_attn_kernel,
        out_shape=[jax.ShapeDtypeStruct((N_TOK, D_MODEL), f32)] + cast_shapes,
        grid=(nsteps,),
        in_specs=[pl.BlockSpec(memory_space=pltpu.SMEM),
                  pl.BlockSpec((N_Q_HEADS, TQ, HEAD_DIM), lambda i: (0, i, 0))]
                 + k_band + v_band + [
            pl.BlockSpec((N_KV_HEADS, HEAD_DIM, CTX_LEN), lambda i: (0, 0, i // tps)),
            pl.BlockSpec((N_KV_HEADS, CTX_LEN, LANE), lambda i: (0, i // tps, 0)),
            pl.BlockSpec((TQ, POOL_WIDTH), lambda i: (i, 0)),
            pl.BlockSpec((TQ, D_MODEL), lambda i: (i, 0)),
            _mods_spec(0),
            _table_spec(g_post),
            _layer_spec(w_out.shape, 0, single_buffer=True),
        ] + cast_specs,
        out_specs=[pl.BlockSpec((TQ, D_MODEL), lambda i: (i, 0))] + cast_specs,
        scratch_shapes=[pltpu.VMEM((TQ, POOL_WIDTH + ATTN_WIDTH), bf16)],
        compiler_params=_params(1),
        name="even_attn",
    )(sink, q, k, k, k, v, v, v, kc, vc, yp, x2d, mods, g_post, w_out, *flat)
    return outs[0], [o.reshape(w.shape) for o, w in zip(outs[1:], later_weights)]


def _conv_ffn_kernel(xm_ref, xp_ref, xn_ref, mods_ref, gpre_ref, gpost_ref, wup_ref, cw_ref, cb_ref,
                     wdn_ref, o_ref, slab_ref, h_ref, act_ref, *, layer):
    tm = TM_FFN
    nr = tm // SUBLANE
    nslab = D_MODEL // LANE
    b, ti, tps = _seq_tile(tm)
    sub = lax.broadcasted_iota(jnp.int32, (SUBLANE, FF_CHUNK), 0)

    def conv(hu, c0):
        w = cw_ref[layer, :, c0:c0 + FF_CHUNK]
        first = jnp.where(sub == 0, pltpu.roll(hu[tm:tm + SUBLANE], 1, axis=0),
                          pltpu.roll(hu[tm - SUBLANE:tm], 1, axis=0))
        last = jnp.where(sub == SUBLANE - 1, pltpu.roll(hu[tm + SUBLANE:], SUBLANE - 1, axis=0),
                         pltpu.roll(hu[:SUBLANE], SUBLANE - 1, axis=0))
        before = jnp.concatenate([first, hu[:tm - SUBLANE]], axis=0)
        after = jnp.concatenate([hu[SUBLANE:tm], last], axis=0)
        return (before * w[0:1] + hu[:tm] * w[1:2] + after * w[2:3]
                + cb_ref[layer:layer + 1, c0:c0 + FF_CHUNK])

    g = gpre_ref[layer:layer + 1, :]
    shift = _mod_row(mods_ref, SH_F, b)
    scale = _mod_row(mods_ref, SC_F, b)
    val = _pre(xm_ref[...], g, shift, scale)
    for j in range(nslab):
        for a in range(SUBLANE):
            slab_ref[j, a * SLAB_PITCH:a * SLAB_PITCH + nr, :] = val[a * nr:(a + 1) * nr, j * LANE:(j + 1) * LANE]
    for r in range(0, nr, 2):
        rows = [jnp.concatenate([slab_ref[j, pl.ds(r + k, SUBLANE, stride=SLAB_PITCH), :]
                                 for j in range(nslab)], axis=1) for k in range(2)]
        h_ref[r * SUBLANE:(r + 2) * SUBLANE, :] = jnp.concatenate(rows, axis=0).astype(bf16)
    hp = jnp.where(ti > 0, _pre(xp_ref[...], g, shift, scale), 0.0)
    hn = jnp.where(ti < tps - 1, _pre(xn_ref[...], g, shift, scale), 0.0)
    h_ref[tm:, :] = jnp.concatenate([hp, hn], axis=0).astype(bf16)
    h = h_ref[...]
    for c in range(D_FF // FF_CHUNK):
        g0 = c * FF_CHUNK
        u0 = D_FF + c * FF_CHUNK
        gate = conv(jnp.dot(h, wup_ref[:, g0:g0 + FF_CHUNK], preferred_element_type=f32), g0)
        up = conv(jnp.dot(h, wup_ref[:, u0:u0 + FF_CHUNK], preferred_element_type=f32), u0)
        act_ref[:, g0:g0 + FF_CHUNK] = (_silu(gate) * up).astype(bf16)

    gain = _mod_row(mods_ref, GT_F, b) * gpost_ref[layer:layer + 1, :]
    nrp = nr // FFN_DOWN_PARTS
    for part in range(FFN_DOWN_PARTS):
        p0 = part * nrp * SUBLANE
        res = _rms(jnp.dot(act_ref[p0:p0 + nrp * SUBLANE, :], wdn_ref[...], preferred_element_type=f32), gain)
        for r in range(nrp):
            for j in range(nslab):
                slab_ref[j, pl.ds(part * nrp + r, SUBLANE, stride=SLAB_PITCH), :] = (
                    res[r * SUBLANE:(r + 1) * SUBLANE, j * LANE:(j + 1) * LANE])
        for j in range(nslab):
            for a in range(SUBLANE):
                rows = slice(a * nr + part * nrp, a * nr + (part + 1) * nrp)
                seg = slice(a * SLAB_PITCH + part * nrp, a * SLAB_PITCH + (part + 1) * nrp)
                lanes = slice(j * LANE, (j + 1) * LANE)
                o_ref[rows, lanes] = xm_ref[rows, lanes] + slab_ref[j, seg, :]


def _conv_ffn(x2d, mods, layer, g_pre, g_post, w_up, conv_w, conv_b, w_down):
    tm = TM_FFN
    return pl.pallas_call(
        functools.partial(_conv_ffn_kernel, layer=layer),
        out_shape=jax.ShapeDtypeStruct((N_TOK, D_MODEL), f32),
        grid=(N_TOK // tm,),
        in_specs=_halo_specs(tm, D_MODEL, SUBLANE) + [
            _mods_spec(layer),
            _table_spec(g_pre),
            _table_spec(g_post),
            _layer_spec(w_up.shape, layer, single_buffer=True),
            _table_spec(conv_w),
            _table_spec(conv_b),
            _layer_spec(w_down.shape, layer, single_buffer=True),
        ],
        out_specs=pl.BlockSpec((tm, D_MODEL), lambda i: (i, 0)),
        scratch_shapes=[pltpu.VMEM((D_MODEL // LANE, SUBLANE * SLAB_PITCH, LANE), f32),
                        pltpu.VMEM((tm + 2 * SUBLANE, D_MODEL), bf16),
                        pltpu.VMEM((tm, D_FF), bf16)],
        compiler_params=_params(1),
        name="conv_ffn",
    )(x2d, x2d, x2d, mods, g_pre, g_post, w_up, conv_w, conv_b, w_down)


def _gelu_tanh(x):
    k = -2.0 * GELU_C * LOG2_E
    return x / (1.0 + jnp.exp2(x * (k + (k * 0.044715) * (x * x))))


def _odd_mixer_kernel(x_ref, mods_ref, gpre_ref, gpost_ref, win_ref, lng_ref, lnb_ref, ws_ref, bs_ref,
                      wout_ref, o_ref, gated_ref):
    tm = TM_ODD
    nch = tm // CHUNK
    gw = D_MODEL // N_SGU_GROUPS
    b, _, _ = _seq_tile(tm)
    x = x_ref[...]
    h = _pre(x, gpre_ref[1:2, :], _mod_row(mods_ref, SH_M, b), _mod_row(mods_ref, SC_M, b)).astype(bf16)
    def proj(lo):
        return jnp.concatenate(
            [_gelu_tanh(jnp.dot(h, win_ref[:, c0:c0 + ODD_CHUNK], preferred_element_type=f32))
             for c0 in range(lo, lo + D_MODEL, ODD_CHUNK)], axis=1)

    v = proj(D_MODEL)
    u = proj(0)
    mu = jnp.mean(v, axis=-1, keepdims=True)
    vc = v - mu
    var = jnp.mean(vc * vc, axis=-1, keepdims=True)
    vn = (vc * lax.rsqrt(var + EPS) * lng_ref[...] + lnb_ref[...]).astype(bf16)
    for g in range(N_SGU_GROUPS):
        lanes = slice(g * gw, (g + 1) * gw)
        vg = jnp.concatenate([vn[n * CHUNK:(n + 1) * CHUNK, lanes] for n in range(nch)], axis=1)
        s = jnp.dot(ws_ref[g], vg, preferred_element_type=f32)
        for n in range(nch):
            rows = slice(n * CHUNK, (n + 1) * CHUNK)
            gated_ref[rows, lanes] = (u[rows, lanes] * (s[:, n * gw:(n + 1) * gw] + bs_ref[g])).astype(bf16)
    y = jnp.dot(gated_ref[...], wout_ref[...], preferred_element_type=f32)
    o_ref[...] = x + _rms(y, _mod_row(mods_ref, GT_M, b) * gpost_ref[1:2, :])


def _odd_mixer(x2d, mods, g_pre, g_post, w_in, ln_g, ln_b, w_s, b_s, w_out):
    tm = TM_ODD
    return pl.pallas_call(
        _odd_mixer_kernel,
        out_shape=jax.ShapeDtypeStruct((N_TOK, D_MODEL), f32),
        grid=(N_TOK // tm,),
        in_specs=[
            pl.BlockSpec((tm, D_MODEL), lambda i: (i, 0)),
            _mods_spec(1),
            _table_spec(g_pre),
            _table_spec(g_post),
            _layer_spec(w_in.shape, 0, single_buffer=True),
            _table_spec(ln_g),
            _table_spec(ln_b),
            _layer_spec(w_s.shape, 0, single_buffer=True),
            pl.BlockSpec(b_s.shape, lambda i: (0, 0, 0), pipeline_mode=pl.Buffered(1)),
            _layer_spec(w_out.shape, 0, single_buffer=True),
        ],
        out_specs=pl.BlockSpec((tm, D_MODEL), lambda i: (i, 0)),
        scratch_shapes=[pltpu.VMEM((tm, D_MODEL), bf16)],
        compiler_params=_params(1),
        name="odd_mixer",
    )(x2d, mods, g_pre, g_post, w_in, ln_g, ln_b, w_s, b_s, w_out)


def _rope_tables():
    pos = np.arange(SEQ)
    inv = (np.float32(ROPE_BASE) ** (-np.arange(ROPE_FREQS, dtype=np.float32) / ROPE_FREQS)).astype(np.float32)
    ar = (pos // GRID_W).astype(np.float32)[:, None] * inv
    ac = (pos % GRID_W).astype(np.float32)[:, None] * inv
    cos = np.concatenate([np.cos(ar), np.cos(ar), np.cos(ac), np.cos(ac)] * 2, axis=1)
    sin = np.concatenate([-np.sin(ar), np.sin(ar), -np.sin(ac), np.sin(ac)] * 2, axis=1)
    q_scale = HEAD_DIM ** -0.5 * LOG2_E
    return np.stack([cos, sin, cos * q_scale, sin * q_scale]).astype(np.float32)


_ROPE_TABLES = _rope_tables()


def kernel(x, c, ctx, c_ctx, w_ada, b_ada, g_mix_pre, g_mix_post, g_ffn_pre, g_ffn_post, w_in_even, w_pool,
           pool_scale, attn_sink, w_out_even, w_in_odd, sgu_ln_g, sgu_ln_b, sgu_w, sgu_b, w_out_odd,
           w_ffn_up, ffn_conv_w, ffn_conv_b, w_ffn_down):
    assert x.shape == (BATCH, SEQ, D_MODEL) and ctx.shape == (BATCH, CTX_LEN, D_MODEL)
    x2d = x.reshape(N_TOK, D_MODEL)
    ctx2d = ctx.reshape(BATCH * CTX_LEN, D_MODEL)

    cvec = jnp.concatenate([c, c_ctx[None, :], jnp.zeros((SUBLANE - BATCH - 1, D_MODEL), f32)], axis=0)
    mods, (w_in, w_pool_b) = _adaln(cvec, w_ada, b_ada, [w_in_even, w_pool])


    kc, vc = _ctx_kv(ctx2d, mods, g_mix_pre, w_in)
    (yp, q, k, v), (w_out_e,) = _even_in(x2d, mods, g_mix_pre, w_in, jnp.asarray(_ROPE_TABLES),
                                          w_pool_b, pool_scale, [w_out_even])
    x2d, (w_up, w_down, w_in_o, w_out_o, w_sgu) = _even_attn(
        attn_sink[0], q, k, v, kc, vc, yp, x2d, mods, g_mix_post, w_out_e,
        [w_ffn_up, w_ffn_down, w_in_odd, w_out_odd, sgu_w])
    x2d = _conv_ffn(x2d, mods, 0, g_ffn_pre, g_ffn_post, w_up, ffn_conv_w, ffn_conv_b, w_down)

    b_s = jnp.broadcast_to(sgu_b[0][:, :, None], (N_SGU_GROUPS, CHUNK, CHUNK))
    x2d = _odd_mixer(x2d, mods, g_mix_pre, g_mix_post, w_in_o, sgu_ln_g, sgu_ln_b, w_sgu, b_s, w_out_o)
    x2d = _conv_ffn(x2d, mods, 1, g_ffn_pre, g_ffn_post, w_up, ffn_conv_w, ffn_conv_b, w_down)
    return x2d.reshape(BATCH, SEQ, D_MODEL)
```

```python
import functools
import math

import numpy as np
import jax
import jax.numpy as jnp
from jax import lax
from jax.experimental import pallas as pl
from jax.experimental.pallas import tpu as pltpu

D_MODEL = 1024
BATCH = 4
SEQ = 4096
GRID_W = 64
CTX_LEN = 256
EPS = 1e-6
NEG_INF = -1e30
N_MOD = 6
POOL_WINDOWS = (2, 4, 8, 16)
POOL_GROUP_DIM = 128
POOL_WIDTH = 512
HEAD_DIM = 64
N_Q_HEADS = 8
N_KV_HEADS = 2
GQA_GROUP = N_Q_HEADS // N_KV_HEADS
ATTN_WIDTH = 512
KV_WIDTH = 128
ATTN_BLOCK = 128
ROPE_BASE = 10000.0
ROPE_FREQS = 16
Q_END = POOL_WIDTH + ATTN_WIDTH
CHUNK = 128
N_SGU_GROUPS = 8
D_FF = 2816
LOG2_E = math.log2(math.e)
GELU_C = math.sqrt(2.0 / math.pi)

LANE = 128
SUBLANE = 8
HALO = 16
VMEM_LIMIT = 56 * 1024 * 1024

N_TOK = BATCH * SEQ
TM_IN = 1024
TQ = 1024
TM_FFN = 1024
TM_ODD = 1024
FF_CHUNK = 256
FFN_DOWN_PARTS = 2
ADALN_SLOTS = 3
ODD_CHUNK = 512
SLAB_PITCH = TM_FFN // SUBLANE + SUBLANE

f32 = jnp.float32
bf16 = jnp.bfloat16

SH_M, SC_M, GT_M, SH_F, SC_F, GT_F = range(N_MOD)
CTX_ROW = BATCH


def _rms(x, g):
    return x * lax.rsqrt(jnp.mean(x * x, axis=-1, keepdims=True) + EPS) * g


def _pre(x, g, shift, scale):
    return _rms(x, g * (1.0 + scale)) + shift


def _silu(x):
    return x / (1.0 + jnp.exp2(x * -LOG2_E))


def _mod_row(mods_ref, slot, row):
    return mods_ref[slot, pl.ds(row, 1), :]


def _layer_spec(shape, layer, single_buffer=False):
    zeros = (0,) * (len(shape) - 1)
    mode = dict(pipeline_mode=pl.Buffered(1)) if single_buffer else {}
    return pl.BlockSpec((None,) + tuple(shape[1:]), lambda *_: (layer,) + zeros, **mode)


def _mods_spec(layer):
    return pl.BlockSpec((None, N_MOD, SUBLANE, D_MODEL), lambda *_: (layer, 0, 0, 0))


def _params(n_axes=1):
    return pltpu.CompilerParams(dimension_semantics=("arbitrary",) * n_axes,
                                vmem_limit_bytes=VMEM_LIMIT)


def _table_spec(a):
    zeros = (0,) * a.ndim
    return pl.BlockSpec(a.shape, lambda *_: zeros)


def _cast_riders(weights, nsteps, step_index):
    flat = [w.reshape(-1, w.shape[-1]) for w in weights]
    specs = [pl.BlockSpec((w.shape[0] // nsteps, w.shape[1]), lambda *i: (step_index(*i), 0)) for w in flat]
    return flat, specs, [jax.ShapeDtypeStruct(w.shape, bf16) for w in flat]


def _run_casts(cast_in, cast_out):
    for src, dst in zip(cast_in, cast_out):
        dst[...] = src[...].astype(bf16)


def _adaln_kernel(c_ref, w_ref, b_ref, *rest):
    n_cast = (len(rest) - 1) // 2
    o_ref = rest[n_cast]
    _run_casts(rest[:n_cast], rest[n_cast + 1:])
    s = jax.nn.silu(c_ref[...]).astype(bf16)
    z = jnp.dot(s, w_ref[...].astype(bf16), preferred_element_type=f32)
    bias = b_ref[pl.ds(pl.program_id(0), 1), :]
    for t in range(ADALN_SLOTS):
        o_ref[t] = z[:, t * D_MODEL:(t + 1) * D_MODEL] + bias[:, t * D_MODEL:(t + 1) * D_MODEL]


def _adaln(cvec, w_ada, b_ada, early_weights):
    depth = w_ada.shape[0]
    per_layer = N_MOD // ADALN_SLOTS
    flat, cast_specs, cast_shapes = _cast_riders(early_weights, depth * per_layer, lambda l, j: l * per_layer + j)
    outs = pl.pallas_call(
        _adaln_kernel,
        out_shape=[jax.ShapeDtypeStruct((depth, N_MOD, SUBLANE, D_MODEL), f32)] + cast_shapes,
        grid=(depth, per_layer),
        in_specs=[
            pl.BlockSpec((SUBLANE, D_MODEL), lambda l, j: (0, 0)),
            pl.BlockSpec((None, D_MODEL, ADALN_SLOTS * D_MODEL), lambda l, j: (l, 0, j)),
            pl.BlockSpec((depth, ADALN_SLOTS * D_MODEL), lambda l, j: (0, j)),
        ] + cast_specs,
        out_specs=[pl.BlockSpec((None, ADALN_SLOTS, SUBLANE, D_MODEL), lambda l, j: (l, j, 0, 0))] + cast_specs,
        compiler_params=_params(2),
        name="adaln",
    )(cvec, w_ada, b_ada, *flat)
    return outs[0], [o.reshape(w.shape) for o, w in zip(outs[1:], early_weights)]


def _store_kv(k, v, k_ref, v_ref):
    low = lax.broadcasted_iota(jnp.int32, v.shape, 1) < HEAD_DIM
    swapped = pltpu.roll(v, HEAD_DIM, axis=1)
    kt = k.T.astype(bf16)
    for head, vals in enumerate((v, swapped)):
        k_ref[head] = kt[head * HEAD_DIM:(head + 1) * HEAD_DIM]
        v_ref[head] = jnp.where(low, vals, 1.0).astype(bf16)


def _ctx_kv(x_ref, mods_ref, g_ref, w_ref, k_ref, v_ref):
    h = _pre(x_ref[...], g_ref[0:1, :], mods_ref[SH_M, CTX_ROW:CTX_ROW + 1, :],
             mods_ref[SC_M, CTX_ROW:CTX_ROW + 1, :]).astype(bf16)
    z = jnp.dot(h, w_ref[:, Q_END:], preferred_element_type=f32)
    _store_kv(z[:, :KV_WIDTH], z[:, KV_WIDTH:], k_ref, v_ref)


def _halo_specs(tm, width, halo=HALO):
    per = tm // halo
    last = N_TOK // halo - 1
    return [
        pl.BlockSpec((tm, width), lambda i: (i, 0)),
        pl.BlockSpec((halo, width), lambda i: (jnp.maximum(i * per - 1, 0), 0)),
        pl.BlockSpec((halo, width), lambda i: (jnp.minimum((i + 1) * per, last), 0)),
    ]


def _seq_tile(tm):
    tps = SEQ // tm
    i = pl.program_id(0)
    return i // tps, i % tps, tps


def _normed_with_halo(xm_ref, xp_ref, xn_ref, g, shift, scale, tm):
    _, ti, tps = _seq_tile(tm)
    hp = _pre(xp_ref[...], g, shift, scale)
    hn = _pre(xn_ref[...], g, shift, scale)
    hp = jnp.where(ti > 0, hp, 0.0)
    hn = jnp.where(ti < tps - 1, hn, 0.0)
    hm = _pre(xm_ref[...], g, shift, scale)
    return jnp.concatenate([hp, hm, hn], axis=0).astype(bf16)


def _rope(z, cos, sin_signed, lane):
    fwd = pltpu.roll(z, LANE - ROPE_FREQS, axis=1)
    bwd = pltpu.roll(z, ROPE_FREQS, axis=1)
    partner = jnp.where((lane % (2 * ROPE_FREQS)) < ROPE_FREQS, fwd, bwd)
    return z * cos + partner * sin_signed


def _even_in_kernel(xm_ref, xp_ref, xn_ref, mods_ref, g_ref, w_ref, rope_ref, wpool_ref, pscale_ref, ctx_ref,
                    *rest):
    n_cast = (len(rest) - 6) // 2
    yp_ref, q_ref, k_ref, v_ref, kc_ref, vc_ref = rest[n_cast:n_cast + 6]
    _run_casts(rest[:n_cast], rest[n_cast + 6:])
    tm = TM_IN
    b, ti, _ = _seq_tile(tm)

    @pl.when(ti == 0)
    def _():
        _ctx_kv(ctx_ref, mods_ref, g_ref, w_ref, kc_ref, vc_ref)

    shift = _mod_row(mods_ref, SH_M, b)
    scale = _mod_row(mods_ref, SC_M, b)
    h = _normed_with_halo(xm_ref, xp_ref, xn_ref, g_ref[0:1, :], shift, scale, tm)
    hm = h[HALO:HALO + tm]
    z = jnp.dot(h, w_ref[:, :POOL_WIDTH], preferred_element_type=f32)
    zq = jnp.dot(hm, w_ref[:, POOL_WIDTH:Q_END], preferred_element_type=f32)
    zkv = jnp.dot(hm, w_ref[:, Q_END:], preferred_element_type=f32)

    rows = tm + 2 * HALO
    edge = lax.broadcasted_iota(jnp.int32, (SUBLANE, POOL_GROUP_DIM), 0) + ti * tm
    for gi, win in enumerate(POOL_WINDOWS):
        lanes = slice(gi * POOL_GROUP_DIM, (gi + 1) * POOL_GROUP_DIM)
        u = z[:, lanes]
        s = u + pltpu.roll(u, 1, axis=0)
        step = 1
        while 2 * step < win:
            s = pltpu.roll(s, step, axis=0) + pltpu.roll(s, rows - step, axis=0)
            step *= 2
        half = win // 2
        count = lambda pos: (jnp.minimum(pos + half, SEQ) - jnp.maximum(pos - half, 0)).astype(f32)
        sm = s[HALO:HALO + tm]
        mean = jnp.concatenate([sm[:SUBLANE] / count(edge),
                                sm[SUBLANE:tm - SUBLANE] * (1.0 / win),
                                sm[tm - SUBLANE:] / count(edge + (tm - SUBLANE))], axis=0)
        pooled = (mean - u[HALO:HALO + tm]).astype(bf16)
        y = jnp.dot(pooled, wpool_ref[gi], preferred_element_type=f32)
        yp_ref[:, lanes] = (y * pscale_ref[:, lanes]).astype(bf16)

    lane = lax.broadcasted_iota(jnp.int32, (tm, LANE), 1)
    for j in range(ATTN_WIDTH // LANE):
        pair = _rope(zq[:, j * LANE:(j + 1) * LANE], rope_ref[2], rope_ref[3], lane)
        q_ref[2 * j] = pair[:, :HEAD_DIM].astype(bf16)
        q_ref[2 * j + 1] = pair[:, HEAD_DIM:].astype(bf16)
    _store_kv(_rope(zkv[:, :KV_WIDTH], rope_ref[0], rope_ref[1], lane), zkv[:, KV_WIDTH:], k_ref, v_ref)


def _even_in(x2d, ctx2d, mods, g, w_in, rope, w_pool, pool_scale, later_weights):
    tm = TM_IN
    tps = SEQ // tm
    n_ctx = ctx2d.shape[0]
    heads = lambda n, w: pl.BlockSpec((n, tm, w), lambda i: (0, i, 0))
    flat, cast_specs, cast_shapes = _cast_riders(later_weights, N_TOK // tm, lambda i: i)
    outs = pl.pallas_call(
        _even_in_kernel,
        out_shape=[jax.ShapeDtypeStruct((N_TOK, POOL_WIDTH), bf16),
                   jax.ShapeDtypeStruct((N_Q_HEADS, N_TOK, HEAD_DIM), bf16),
                   jax.ShapeDtypeStruct((N_KV_HEADS, HEAD_DIM, N_TOK), bf16),
                   jax.ShapeDtypeStruct((N_KV_HEADS, N_TOK, LANE), bf16),
                   jax.ShapeDtypeStruct((N_KV_HEADS, HEAD_DIM, n_ctx), bf16),
                   jax.ShapeDtypeStruct((N_KV_HEADS, n_ctx, LANE), bf16)] + cast_shapes,
        grid=(N_TOK // tm,),
        in_specs=_halo_specs(tm, D_MODEL) + [
            _mods_spec(0),
            _table_spec(g),
            _layer_spec(w_in.shape, 0, single_buffer=True),
            pl.BlockSpec((rope.shape[0], tm, LANE), lambda i: (0, i % tps, 0)),
            _layer_spec(w_pool.shape, 0, single_buffer=True),
            _table_spec(pool_scale),
            pl.BlockSpec((CTX_LEN, D_MODEL), lambda i: (i // tps, 0)),
        ] + cast_specs,
        out_specs=[pl.BlockSpec((tm, POOL_WIDTH), lambda i: (i, 0)), heads(N_Q_HEADS, HEAD_DIM),
                   pl.BlockSpec((N_KV_HEADS, HEAD_DIM, tm), lambda i: (0, 0, i)), heads(N_KV_HEADS, LANE),
                   pl.BlockSpec((N_KV_HEADS, HEAD_DIM, CTX_LEN), lambda i: (0, 0, i // tps)),
                   pl.BlockSpec((N_KV_HEADS, CTX_LEN, LANE), lambda i: (0, i // tps, 0))]
                  + cast_specs,
        compiler_params=_params(1),
        name="even_in",
    )(x2d, x2d, x2d, mods, g, w_in, rope, w_pool, pool_scale, ctx2d, *flat)
    return outs[:6], [o.reshape(w.shape) for o, w in zip(outs[6:], later_weights)]


def _even_attn_kernel(sink_ref, q_ref, km_ref, kp_ref, kn_ref, vm_ref, vp_ref, vn_ref,
                      kc_ref, vc_ref, yp_ref, x_ref, mods_ref, g_ref, wout_ref, *rest):
    n_cast = (len(rest) - 2) // 2
    o_ref, mix_ref = rest[n_cast], rest[-1]
    _run_casts(rest[:n_cast], rest[n_cast + 1:-1])
    b, ti, tps = _seq_tile(TQ)
    nblk = TQ // ATTN_BLOCK

    def row_blocks(main_ref, prev_ref, next_ref, ctx_ref, g):
        blocks = [prev_ref[g]]
        blocks += [main_ref[g, j * ATTN_BLOCK:(j + 1) * ATTN_BLOCK, :] for j in range(nblk)]
        return blocks + [next_ref[g], ctx_ref[g]]

    ri = lax.broadcasted_iota(jnp.int32, (ATTN_BLOCK, ATTN_BLOCK), 0)
    ci = lax.broadcasted_iota(jnp.int32, (ATTN_BLOCK, ATTN_BLOCK), 1)
    lower_ok = ci >= ri
    upper_ok = ci <= ri
    first_ok = jnp.logical_and(lower_ok, ti > 0)
    last_ok = jnp.logical_and(upper_ok, ti < tps - 1)
    low = lax.broadcasted_iota(jnp.int32, (ATTN_BLOCK, LANE), 1) < HEAD_DIM

    mix_ref[:, :POOL_WIDTH] = yp_ref[...]
    for g in range(N_KV_HEADS):
        kblocks = ([kp_ref[g]] + [km_ref[g, :, j * ATTN_BLOCK:(j + 1) * ATTN_BLOCK] for j in range(nblk)]
                   + [kn_ref[g], kc_ref[g]])
        vblocks = row_blocks(vm_ref, vp_ref, vn_ref, vc_ref, g)
        for j in range(nblk):
            sel = [j, j + 1, j + 2, nblk + 2]
            rows = slice(j * ATTN_BLOCK, (j + 1) * ATTN_BLOCK)
            keys = jnp.concatenate([kblocks[s] for s in sel], axis=1)
            vals = jnp.concatenate([vblocks[s] for s in sel], axis=0)
            qg = jnp.concatenate([q_ref[GQA_GROUP * g + hh, rows, :] for hh in range(GQA_GROUP)], axis=0)
            s_all = jnp.dot(qg, keys, preferred_element_type=f32)
            m_prev = first_ok if j == 0 else lower_ok
            m_next = last_ok if j == nblk - 1 else upper_ok
            p_rows = []
            sink_terms = []
            for hh in range(GQA_GROUP):
                sink = sink_ref[GQA_GROUP * g + hh] * LOG2_E
                s = s_all[hh * ATTN_BLOCK:(hh + 1) * ATTN_BLOCK]
                s = jnp.concatenate([
                    jnp.where(m_prev, s[:, :ATTN_BLOCK], NEG_INF),
                    s[:, ATTN_BLOCK:2 * ATTN_BLOCK],
                    jnp.where(m_next, s[:, 2 * ATTN_BLOCK:3 * ATTN_BLOCK], NEG_INF),
                    s[:, 3 * ATTN_BLOCK:]], axis=1)
                m = jnp.maximum(jnp.max(s, axis=-1, keepdims=True), sink)
                p_rows.append(jnp.exp2(s - m).astype(bf16))
                sink_terms.append(jnp.exp2(sink - m))
            out = jnp.dot(jnp.concatenate(p_rows, axis=0), vals, preferred_element_type=f32)
            for pr in range(GQA_GROUP // 2):
                even = out[(2 * pr) * ATTN_BLOCK:(2 * pr + 1) * ATTN_BLOCK]
                odd = out[(2 * pr + 1) * ATTN_BLOCK:(2 * pr + 2) * ATTN_BLOCK]
                o = jnp.where(low, even / (pltpu.roll(even, HEAD_DIM, axis=1) + sink_terms[2 * pr]),
                              pltpu.roll(odd, HEAD_DIM, axis=1) / (odd + sink_terms[2 * pr + 1]))
                c0 = POOL_WIDTH + (2 * g + pr) * LANE
                mix_ref[rows, c0:c0 + LANE] = o.astype(bf16)

    y = jnp.dot(mix_ref[...], wout_ref[...], preferred_element_type=f32)
    o_ref[...] = x_ref[...] + _rms(y, _mod_row(mods_ref, GT_M, b) * g_ref[0:1, :])


def _even_attn(sink, q, k, v, kc, vc, yp, x2d, mods, g_post, w_out, later_weights):
    nsteps = N_TOK // TQ
    flat, cast_specs, cast_shapes = _cast_riders(later_weights, nsteps, lambda i: i)
    per = TQ // ATTN_BLOCK
    last = N_TOK // ATTN_BLOCK - 1
    tps = SEQ // TQ
    prev_blk = lambda i: jnp.maximum(i * per - 1, 0)
    next_blk = lambda i: jnp.minimum((i + 1) * per, last)
    k_band = [
        pl.BlockSpec((N_KV_HEADS, HEAD_DIM, TQ), lambda i: (0, 0, i)),
        pl.BlockSpec((N_KV_HEADS, HEAD_DIM, ATTN_BLOCK), lambda i: (0, 0, prev_blk(i))),
        pl.BlockSpec((N_KV_HEADS, HEAD_DIM, ATTN_BLOCK), lambda i: (0, 0, next_blk(i))),
    ]
    v_band = [
        pl.BlockSpec((N_KV_HEADS, TQ, LANE), lambda i: (0, i, 0)),
        pl.BlockSpec((N_KV_HEADS, ATTN_BLOCK, LANE), lambda i: (0, prev_blk(i), 0)),
        pl.BlockSpec((N_KV_HEADS, ATTN_BLOCK, LANE), lambda i: (0, next_blk(i), 0)),
    ]
    outs = pl.pallas_call(
        _even_attn_kernel,
        out_shape=[jax.ShapeDtypeStruct((N_TOK, D_MODEL), f32)] + cast_shapes,
        grid=(nsteps,),
        in_specs=[pl.BlockSpec(memory_space=pltpu.SMEM),
                  pl.BlockSpec((N_Q_HEADS, TQ, HEAD_DIM), lambda i: (0, i, 0))]
                 + k_band + v_band + [
            pl.BlockSpec((N_KV_HEADS, HEAD_DIM, CTX_LEN), lambda i: (0, 0, i // tps)),
            pl.BlockSpec((N_KV_HEADS, CTX_LEN, LANE), lambda i: (0, i // tps, 0)),
            pl.BlockSpec((TQ, POOL_WIDTH), lambda i: (i, 0)),
            pl.BlockSpec((TQ, D_MODEL), lambda i: (i, 0)),
            _mods_spec(0),
            _table_spec(g_post),
            _layer_spec(w_out.shape, 0, single_buffer=True),
        ] + cast_specs,
        out_specs=[pl.BlockSpec((TQ, D_MODEL), lambda i: (i, 0))] + cast_specs,
        scratch_shapes=[pltpu.VMEM((TQ, POOL_WIDTH + ATTN_WIDTH), bf16)],
        compiler_params=_params(1),
        name="even_attn",
    )(sink, q, k, k, k, v, v, v, kc, vc, yp, x2d, mods, g_post, w_out, *flat)
    return outs[0], [o.reshape(w.shape) for o, w in zip(outs[1:], later_weights)]


def _conv_ffn_kernel(xm_ref, xp_ref, xn_ref, mods_ref, gpre_ref, gpost_ref, wup_ref, cw_ref, cb_ref,
                     wdn_ref, o_ref, slab_ref, h_ref, act_ref, *, layer):
    tm = TM_FFN
    nr = tm // SUBLANE
    nslab = D_MODEL // LANE
    b, ti, tps = _seq_tile(tm)
    sub = lax.broadcasted_iota(jnp.int32, (SUBLANE, FF_CHUNK), 0)

    def conv(hu, c0):
        w = cw_ref[layer, :, c0:c0 + FF_CHUNK]
        first = jnp.where(sub == 0, pltpu.roll(hu[tm:tm + SUBLANE], 1, axis=0),
                          pltpu.roll(hu[tm - SUBLANE:tm], 1, axis=0))
        last = jnp.where(sub == SUBLANE - 1, pltpu.roll(hu[tm + SUBLANE:], SUBLANE - 1, axis=0),
                         pltpu.roll(hu[:SUBLANE], SUBLANE - 1, axis=0))
        before = jnp.concatenate([first, hu[:tm - SUBLANE]], axis=0)
        after = jnp.concatenate([hu[SUBLANE:tm], last], axis=0)
        return (before * w[0:1] + hu[:tm] * w[1:2] + after * w[2:3]
                + cb_ref[layer:layer + 1, c0:c0 + FF_CHUNK])

    g = gpre_ref[layer:layer + 1, :]
    shift = _mod_row(mods_ref, SH_F, b)
    scale = _mod_row(mods_ref, SC_F, b)
    val = _pre(xm_ref[...], g, shift, scale)
    for j in range(nslab):
        for a in range(SUBLANE):
            slab_ref[j, a * SLAB_PITCH:a * SLAB_PITCH + nr, :] = val[a * nr:(a + 1) * nr, j * LANE:(j + 1) * LANE]
    for r in range(0, nr, 2):
        rows = [jnp.concatenate([slab_ref[j, pl.ds(r + k, SUBLANE, stride=SLAB_PITCH), :]
                                 for j in range(nslab)], axis=1) for k in range(2)]
        h_ref[r * SUBLANE:(r + 2) * SUBLANE, :] = jnp.concatenate(rows, axis=0).astype(bf16)
    hp = jnp.where(ti > 0, _pre(xp_ref[...], g, shift, scale), 0.0)
    hn = jnp.where(ti < tps - 1, _pre(xn_ref[...], g, shift, scale), 0.0)
    h_ref[tm:, :] = jnp.concatenate([hp, hn], axis=0).astype(bf16)
    h = h_ref[...]
    for c in range(D_FF // FF_CHUNK):
        g0 = c * FF_CHUNK
        u0 = D_FF + c * FF_CHUNK
        gate = conv(jnp.dot(h, wup_ref[:, g0:g0 + FF_CHUNK], preferred_element_type=f32), g0)
        up = conv(jnp.dot(h, wup_ref[:, u0:u0 + FF_CHUNK], preferred_element_type=f32), u0)
        act_ref[:, g0:g0 + FF_CHUNK] = (_silu(gate) * up).astype(bf16)

    gain = _mod_row(mods_ref, GT_F, b) * gpost_ref[layer:layer + 1, :]
    nrp = nr // FFN_DOWN_PARTS
    for part in range(FFN_DOWN_PARTS):
        p0 = part * nrp * SUBLANE
        res = _rms(jnp.dot(act_ref[p0:p0 + nrp * SUBLANE, :], wdn_ref[...], preferred_element_type=f32), gain)
        for r in range(nrp):
            for j in range(nslab):
                slab_ref[j, pl.ds(part * nrp + r, SUBLANE, stride=SLAB_PITCH), :] = (
                    res[r * SUBLANE:(r + 1) * SUBLANE, j * LANE:(j + 1) * LANE])
        for j in range(nslab):
            for a in range(SUBLANE):
                rows = slice(a * nr + part * nrp, a * nr + (part + 1) * nrp)
                seg = slice(a * SLAB_PITCH + part * nrp, a * SLAB_PITCH + (part + 1) * nrp)
                lanes = slice(j * LANE, (j + 1) * LANE)
                o_ref[rows, lanes] = xm_ref[rows, lanes] + slab_ref[j, seg, :]


def _conv_ffn(x2d, mods, layer, g_pre, g_post, w_up, conv_w, conv_b, w_down):
    tm = TM_FFN
    return pl.pallas_call(
        functools.partial(_conv_ffn_kernel, layer=layer),
        out_shape=jax.ShapeDtypeStruct((N_TOK, D_MODEL), f32),
        grid=(N_TOK // tm,),
        in_specs=_halo_specs(tm, D_MODEL, SUBLANE) + [
            _mods_spec(layer),
            _table_spec(g_pre),
            _table_spec(g_post),
            _layer_spec(w_up.shape, layer, single_buffer=True),
            _table_spec(conv_w),
            _table_spec(conv_b),
            _layer_spec(w_down.shape, layer, single_buffer=True),
        ],
        out_specs=pl.BlockSpec((tm, D_MODEL), lambda i: (i, 0)),
        scratch_shapes=[pltpu.VMEM((D_MODEL // LANE, SUBLANE * SLAB_PITCH, LANE), f32),
                        pltpu.VMEM((tm + 2 * SUBLANE, D_MODEL), bf16),
                        pltpu.VMEM((tm, D_FF), bf16)],
        compiler_params=_params(1),
        name="conv_ffn",
    )(x2d, x2d, x2d, mods, g_pre, g_post, w_up, conv_w, conv_b, w_down)


def _gelu_tanh(x):
    k = -2.0 * GELU_C * LOG2_E
    return x / (1.0 + jnp.exp2(x * (k + (k * 0.044715) * (x * x))))


def _odd_mixer_kernel(x_ref, mods_ref, gpre_ref, gpost_ref, win_ref, lng_ref, lnb_ref, ws_ref, bs_ref,
                      wout_ref, o_ref, gated_ref):
    tm = TM_ODD
    nch = tm // CHUNK
    gw = D_MODEL // N_SGU_GROUPS
    b, _, _ = _seq_tile(tm)
    x = x_ref[...]
    h = _pre(x, gpre_ref[1:2, :], _mod_row(mods_ref, SH_M, b), _mod_row(mods_ref, SC_M, b)).astype(bf16)
    def proj(lo):
        return jnp.concatenate(
            [_gelu_tanh(jnp.dot(h, win_ref[:, c0:c0 + ODD_CHUNK], preferred_element_type=f32))
             for c0 in range(lo, lo + D_MODEL, ODD_CHUNK)], axis=1)

    v = proj(D_MODEL)
    u = proj(0)
    mu = jnp.mean(v, axis=-1, keepdims=True)
    vc = v - mu
    var = jnp.mean(vc * vc, axis=-1, keepdims=True)
    vn = (vc * lax.rsqrt(var + EPS) * lng_ref[...] + lnb_ref[...]).astype(bf16)
    for g in range(N_SGU_GROUPS):
        lanes = slice(g * gw, (g + 1) * gw)
        vg = jnp.concatenate([vn[n * CHUNK:(n + 1) * CHUNK, lanes] for n in range(nch)], axis=1)
        s = jnp.dot(ws_ref[g], vg, preferred_element_type=f32)
        for n in range(nch):
            rows = slice(n * CHUNK, (n + 1) * CHUNK)
            gated_ref[rows, lanes] = (u[rows, lanes] * (s[:, n * gw:(n + 1) * gw] + bs_ref[g])).astype(bf16)
    y = jnp.dot(gated_ref[...], wout_ref[...], preferred_element_type=f32)
    o_ref[...] = x + _rms(y, _mod_row(mods_ref, GT_M, b) * gpost_ref[1:2, :])


def _odd_mixer(x2d, mods, g_pre, g_post, w_in, ln_g, ln_b, w_s, b_s, w_out):
    tm = TM_ODD
    return pl.pallas_call(
        _odd_mixer_kernel,
        out_shape=jax.ShapeDtypeStruct((N_TOK, D_MODEL), f32),
        grid=(N_TOK // tm,),
        in_specs=[
            pl.BlockSpec((tm, D_MODEL), lambda i: (i, 0)),
            _mods_spec(1),
            _table_spec(g_pre),
            _table_spec(g_post),
            _layer_spec(w_in.shape, 0, single_buffer=True),
            _table_spec(ln_g),
            _table_spec(ln_b),
            _layer_spec(w_s.shape, 0, single_buffer=True),
            pl.BlockSpec(b_s.shape, lambda i: (0, 0, 0), pipeline_mode=pl.Buffered(1)),
            _layer_spec(w_out.shape, 0, single_buffer=True),
        ],
        out_specs=pl.BlockSpec((tm, D_MODEL), lambda i: (i, 0)),
        scratch_shapes=[pltpu.VMEM((tm, D_MODEL), bf16)],
        compiler_params=_params(1),
        name="odd_mixer",
    )(x2d, mods, g_pre, g_post, w_in, ln_g, ln_b, w_s, b_s, w_out)


def _rope_tables():
    pos = np.arange(SEQ)
    inv = (np.float32(ROPE_BASE) ** (-np.arange(ROPE_FREQS, dtype=np.float32) / ROPE_FREQS)).astype(np.float32)
    ar = (pos // GRID_W).astype(np.float32)[:, None] * inv
    ac = (pos % GRID_W).astype(np.float32)[:, None] * inv
    cos = np.concatenate([np.cos(ar), np.cos(ar), np.cos(ac), np.cos(ac)] * 2, axis=1)
    sin = np.concatenate([-np.sin(ar), np.sin(ar), -np.sin(ac), np.sin(ac)] * 2, axis=1)
    q_scale = HEAD_DIM ** -0.5 * LOG2_E
    return np.stack([cos, sin, cos * q_scale, sin * q_scale]).astype(np.float32)


_ROPE_TABLES = _rope_tables()


def kernel(x, c, ctx, c_ctx, w_ada, b_ada, g_mix_pre, g_mix_post, g_ffn_pre, g_ffn_post, w_in_even, w_pool,
           pool_scale, attn_sink, w_out_even, w_in_odd, sgu_ln_g, sgu_ln_b, sgu_w, sgu_b, w_out_odd,
           w_ffn_up, ffn_conv_w, ffn_conv_b, w_ffn_down):
    assert x.shape == (BATCH, SEQ, D_MODEL) and ctx.shape == (BATCH, CTX_LEN, D_MODEL)
    x2d = x.reshape(N_TOK, D_MODEL)
    ctx2d = ctx.reshape(BATCH * CTX_LEN, D_MODEL)

    cvec = jnp.concatenate([c, c_ctx[None, :], jnp.zeros((SUBLANE - BATCH - 1, D_MODEL), f32)], axis=0)
    mods, (w_in, w_pool_b) = _adaln(cvec, w_ada, b_ada, [w_in_even, w_pool])


    (yp, q, k, v, kc, vc), (w_out_e,) = _even_in(x2d, ctx2d, mods, g_mix_pre, w_in, jnp.asarray(_ROPE_TABLES),
                                                  w_pool_b, pool_scale, [w_out_even])
    x2d, (w_up, w_down, w_in_o, w_out_o, w_sgu) = _even_attn(
        attn_sink[0], q, k, v, kc, vc, yp, x2d, mods, g_mix_post, w_out_e,
        [w_ffn_up, w_ffn_down, w_in_odd, w_out_odd, sgu_w])
    x2d = _conv_ffn(x2d, mods, 0, g_ffn_pre, g_ffn_post, w_up, ffn_conv_w, ffn_conv_b, w_down)

    b_s = jnp.broadcast_to(sgu_b[0][:, :, None], (N_SGU_GROUPS, CHUNK, CHUNK))
    x2d = _odd_mixer(x2d, mods, g_mix_pre, g_mix_post, w_in_o, sgu_ln_g, sgu_ln_b, w_sgu, b_s, w_out_o)
    x2d = _conv_ffn(x2d, mods, 1, g_ffn_pre, g_ffn_post, w_up, ffn_conv_w, ffn_conv_b, w_down)
    return x2d.reshape(BATCH, SEQ, D_MODEL)
```
